```python
import math
import jax
import jax.numpy as jnp
from jax import lax
import numpy as np

D_MODEL = 1024
BATCH = 8
SEQ = 4096
DEPTH = 1
DEC_BATCH = 2
DEC_SEQ = 8192
PAST_LEN = 128

N_HEADS_ATTN = 8
HEAD_DIM_ATTN = 64
D_ATTN = N_HEADS_ATTN * HEAD_DIM_ATTN
DILATED_PAIRS = ((128, 1), (512, 4), (2048, 16))
HALF_KEYS = 64
NUM_BUCKETS = 32
MAX_DISTANCE = 1024
N_HEADS_HGRN = 4
HEAD_DIM_HGRN = 128
D_HGRN = N_HEADS_HGRN * HEAD_DIM_HGRN
HGRN_CHUNK = 64
D_MIX = D_ATTN + D_HGRN
IN_SPLITS = (D_ATTN, D_ATTN, D_ATTN, D_HGRN, D_HGRN, D_HGRN, D_HGRN, D_HGRN)
D_IN = 3 * D_ATTN + 5 * D_HGRN
N_EXPERTS = 32
TOP_K = 4
D_FF_EXPERT = 1024
SWIGLU_LIMIT = 7.0
SWIGLU_ALPHA = 1.702
DEEPNORM_ALPHA = (2.0 * DEPTH) ** 0.25
DEEPNORM_BETA = (8.0 * DEPTH) ** -0.25
LN_EPS = 1e-5
NORM_EPS = 1e-6

kernel_name = 'hybrid_hgrn2_dilated_attn_moe_encoder'


def layer_norm(x, g, b):
    xf = x.astype(jnp.float32)
    mu = jnp.mean(xf, axis=-1, keepdims=True)
    var = jnp.mean(jnp.square(xf - mu), axis=-1, keepdims=True)
    y = (xf - mu) * lax.rsqrt(var + LN_EPS) * g.astype(jnp.float32) + b.astype(jnp.float32)
    return y.astype(x.dtype)


def t5_bucket(rel):
    half = NUM_BUCKETS // 2
    ret = jnp.where(rel > 0, half, 0)
    n = jnp.abs(rel)
    max_exact = half // 2
    nf = jnp.maximum(n, 1).astype(jnp.float32)
    large = max_exact + (jnp.log(nf / max_exact) / math.log(MAX_DISTANCE / max_exact)
                         * (half - max_exact)).astype(jnp.int32)
    large = jnp.minimum(large, half - 1)
    return ret + jnp.where(n < max_exact, n, large)


def dilated_window_attention(q, k, v, rel_bias, dil):
    B, S, H, Dh = q.shape
    W = HALF_KEYS
    L = S // dil
    nb = -(-L // W)
    Lp = nb * W

    def to_sub(t):
        return t.reshape(B, L, dil, H, Dh).transpose(0, 2, 3, 1, 4)

    qs = jnp.pad(to_sub(q), ((0, 0), (0, 0), (0, 0), (0, Lp - L), (0, 0))).reshape(B, dil, H, nb, W, Dh)

    def key_blocks(t):
        tp = jnp.pad(to_sub(t), ((0, 0), (0, 0), (0, 0), (W, Lp - L + W), (0, 0)))
        tp = tp.reshape(B, dil, H, nb + 2, W, Dh)
        return jnp.concatenate([tp[:, :, :, :-2], tp[:, :, :, 1:-1], tp[:, :, :, 2:]], axis=4)

    kb = key_blocks(k)
    vb = key_blocks(v)
    qi = jnp.arange(W)[:, None]
    kj = jnp.arange(3 * W)[None, :]
    rel = kj - W - qi
    bias = jnp.transpose(rel_bias[t5_bucket(rel * dil)], (2, 0, 1)).astype(jnp.float32)
    kpos = jnp.arange(nb)[:, None] * W - W + jnp.arange(3 * W)[None, :]
    valid = (jnp.abs(rel) <= W)[None] & ((kpos >= 0) & (kpos < L))[:, None, :]

    s = jnp.einsum('bdhnqc,bdhnkc->bdhnqk', qs, kb).astype(jnp.float32) * (HEAD_DIM_ATTN ** -0.5)
    s = s + bias[None, None, :, None]
    s = jnp.where(valid[None, None, None], s, -jnp.inf)
    m = jnp.max(s, axis=-1)
    p = jnp.exp(s - m[..., None])
    den = jnp.sum(p, axis=-1)
    o = jnp.einsum('bdhnqk,bdhnkc->bdhnqc', p.astype(vb.dtype), vb).astype(jnp.float32) / den[..., None]

    o = o.reshape(B, dil, H, Lp, Dh)[:, :, :, :L].transpose(0, 3, 1, 2, 4).reshape(B, S, H, Dh)
    m = m.reshape(B, dil, H, Lp)[:, :, :, :L].transpose(0, 3, 1, 2).reshape(B, S, H)
    den = den.reshape(B, dil, H, Lp)[:, :, :, :L].transpose(0, 3, 1, 2).reshape(B, S, H)
    return o, m, den


def hgrn2_chunk_scan(q, k, v, log_f):
    B, H, S, K = q.shape
    V = v.shape[-1]
    C = HGRN_CHUNK
    nc = S // C
    q = q.reshape(B, H, nc, C, K)
    k = k.reshape(B, H, nc, C, K)
    v = v.reshape(B, H, nc, C, V)
    b = jnp.cumsum(log_f.reshape(B, H, nc, C, K), axis=3)
    b_last = b[:, :, :, -1:, :]
    qd = q * jnp.exp(b) * (K ** -0.5)
    kd = k * jnp.exp(-b)
    kl = k * jnp.exp(b_last - b)
    causal = jnp.tril(jnp.ones((C, C), dtype=bool))
    a = jnp.where(causal, jnp.einsum('bhnik,bhnjk->bhnij', qd, kd), 0.0)
    o_intra = jnp.einsum('bhnij,bhnjv->bhniv', a, v)
    decay = jnp.exp(b_last[:, :, :, 0, :])

    def step(state, inp):
        qd_n, kl_n, v_n, dec_n = inp
        o_n = jnp.einsum('bhik,bhkv->bhiv', qd_n, state)
        state = dec_n[..., None] * state + jnp.einsum('bhik,bhiv->bhkv', kl_n, v_n)
        return state, o_n

    init = jnp.zeros((B, H, K, V), jnp.float32)
    mv = lambda t: jnp.moveaxis(t, 2, 0)
    _, o_inter = lax.scan(step, init, (mv(qd), mv(kl), mv(v), mv(decay)))
    o_inter = jnp.moveaxis(o_inter, 0, 2)
    return (o_intra + o_inter).reshape(B, H, S, V)


def token_mixer(x, w_in, w_out, rel_bias, lb_fwd, lb_bwd, norm_g):
    B, S, _ = x.shape
    h = jnp.einsum('bsd,de->bse', x, w_in)
    offsets = [int(o) for o in np.cumsum(IN_SPLITS)[:-1]]
    qa, ka, va, qh, zf, zb, ih, gh = jnp.split(h, offsets, axis=-1)

    split_heads = lambda t: t.reshape(B, S, N_HEADS_ATTN, HEAD_DIM_ATTN)
    qa, ka, va = split_heads(qa), split_heads(ka), split_heads(va)
    outs = [dilated_window_attention(qa, ka, va, rel_bias, d) for _, d in DILATED_PAIRS]
    o_all = jnp.stack([o for o, _, _ in outs])
    m_all = jnp.stack([m for _, m, _ in outs])
    den_all = jnp.stack([dn for _, _, dn in outs])
    m_star = jnp.max(m_all, axis=0)
    wts = den_all * jnp.exp(m_all - m_star)
    attn = jnp.sum(wts[..., None] * o_all, axis=0) / jnp.sum(wts, axis=0)[..., None]
    attn = attn.reshape(B, S, D_ATTN).astype(x.dtype)

    def heads(t):
        return t.reshape(B, S, N_HEADS_HGRN, HEAD_DIM_HGRN).transpose(0, 2, 1, 3).astype(jnp.float32)

    q, v, g = heads(qh), heads(ih), heads(gh)

    def forget(z, lb):
        lb = lb.reshape(1, N_HEADS_HGRN, 1, HEAD_DIM_HGRN)
        f = lb + (1.0 - lb) * jax.nn.sigmoid(heads(z))
        return 1.0 - f, jnp.log(f)

    k_f, lf_f = forget(zf, lb_fwd)
    k_b, lf_b = forget(zb, lb_bwd)
    flip = lambda t: jnp.flip(t, axis=2)
    o = hgrn2_chunk_scan(q, k_f, v, lf_f) + flip(hgrn2_chunk_scan(flip(q), flip(k_b), flip(v), flip(lf_b)))
    o = o * lax.rsqrt(jnp.mean(jnp.square(o), axis=-1, keepdims=True) + NORM_EPS)
    o = o * norm_g.astype(jnp.float32) * jax.nn.silu(g)
    hgrn = o.transpose(0, 2, 1, 3).reshape(B, S, D_HGRN).astype(x.dtype)

    return jnp.einsum('bse,ed->bsd', jnp.concatenate([attn, hgrn], axis=-1), w_out)


def moe(x, w_router, b_router, w_gate, b_gate, w_up, b_up, w_down, b_down):
    B, S, D = x.shape
    t = x.reshape(B * S, D)
    logits = (t @ w_router + b_router).astype(jnp.float32)
    top_val, top_idx = lax.top_k(logits, TOP_K)
    top_w = jax.nn.softmax(top_val, axis=-1)
    gates = jnp.sum(jax.nn.one_hot(top_idx, N_EXPERTS, dtype=jnp.float32) * top_w[..., None], axis=1)
    y = jnp.zeros((B * S, D), jnp.float32)
    for e in range(N_EXPERTS):
        gt = jnp.minimum(t @ w_gate[e] + b_gate[e], SWIGLU_LIMIT)
        up = jnp.clip(t @ w_up[e] + b_up[e], -SWIGLU_LIMIT, SWIGLU_LIMIT)
        hdn = (up + 1.0) * gt * jax.nn.sigmoid(SWIGLU_ALPHA * gt)
        y = y + gates[:, e:e + 1] * (hdn @ w_down[e] + b_down[e])
    return y.reshape(B, S, D).astype(x.dtype)


def setup_inputs(seed: int = 0) -> dict:
    key = jax.random.key(seed)
    ks = jax.random.split(key, 24)
    f32 = jnp.float32
    nrm = lambda k, shape, s: jax.random.normal(k, shape, f32) * s
    col_scale = jnp.concatenate([
        jnp.ones((2 * D_ATTN,), f32), jnp.full((D_ATTN,), DEEPNORM_BETA, f32),
        jnp.ones((3 * D_HGRN,), f32), jnp.full((D_HGRN,), DEEPNORM_BETA, f32),
        jnp.ones((D_HGRN,), f32)])
    return {
        'x_prompt': nrm(ks[0], (BATCH, SEQ, D_MODEL), 1.0),
        'x_sample': nrm(ks[1], (DEC_BATCH, DEC_SEQ, D_MODEL), 1.0),
        'w_in': nrm(ks[2], (DEPTH, D_MODEL, D_IN), D_MODEL ** -0.5) * col_scale,
        'w_out': nrm(ks[3], (DEPTH, D_MIX, D_MODEL), D_MIX ** -0.5 * DEEPNORM_BETA),
        'rel_bias': nrm(ks[4], (NUM_BUCKETS, N_HEADS_ATTN), 0.5),
        'hgrn_lb_fwd': nrm(ks[5], (DEPTH + 1, D_HGRN), 0.5),
        'hgrn_lb_bwd': nrm(ks[6], (DEPTH + 1, D_HGRN), 0.5),
        'hgrn_norm_g': 1.0 + nrm(ks[7], (DEPTH, HEAD_DIM_HGRN), 0.02),
        'ln1_g': 1.0 + nrm(ks[8], (DEPTH, D_MODEL), 0.02),
        'ln1_b': nrm(ks[9], (DEPTH, D_MODEL), 0.02),
        'w_router': nrm(ks[10], (DEPTH, D_MODEL, N_EXPERTS), D_MODEL ** -0.5),
        'b_router': nrm(ks[11], (DEPTH, N_EXPERTS), 0.01),
        'w_e_gate': nrm(ks[12], (DEPTH, N_EXPERTS, D_MODEL, D_FF_EXPERT), D_MODEL ** -0.5 * DEEPNORM_BETA),
        'b_e_gate': nrm(ks[13], (DEPTH, N_EXPERTS, D_FF_EXPERT), 0.01),
        'w_e_up': nrm(ks[14], (DEPTH, N_EXPERTS, D_MODEL, D_FF_EXPERT), D_MODEL ** -0.5 * DEEPNORM_BETA),
        'b_e_up': nrm(ks[15], (DEPTH, N_EXPERTS, D_FF_EXPERT), 0.01),
        'w_e_down': nrm(ks[16], (DEPTH, N_EXPERTS, D_FF_EXPERT, D_MODEL), D_FF_EXPERT ** -0.5 * DEEPNORM_BETA),
        'b_e_down': nrm(ks[17], (DEPTH, N_EXPERTS, D_MODEL), 0.01),
        'ln2_g': 1.0 + nrm(ks[18], (DEPTH, D_MODEL), 0.02),
        'ln2_b': nrm(ks[19], (DEPTH, D_MODEL), 0.02),
    }


def reference(x_prompt, x_sample, w_in, w_out, rel_bias, hgrn_lb_fwd, hgrn_lb_bwd, hgrn_norm_g,
              ln1_g, ln1_b, w_router, b_router, w_e_gate, b_e_gate, w_e_up, b_e_up,
              w_e_down, b_e_down, ln2_g, ln2_b):
    lb_fwd_all = jnp.cumsum(jax.nn.softmax(hgrn_lb_fwd.astype(jnp.float32), axis=0), axis=0)
    lb_bwd_all = jnp.cumsum(jax.nn.softmax(hgrn_lb_bwd.astype(jnp.float32), axis=0), axis=0)

    def layer(x, l):
        mix = token_mixer(x, w_in[l], w_out[l], rel_bias, lb_fwd_all[l], lb_bwd_all[l], hgrn_norm_g[l])
        x = layer_norm(DEEPNORM_ALPHA * x + mix, ln1_g[l], ln1_b[l])
        ffn = moe(x, w_router[l], b_router[l], w_e_gate[l], b_e_gate[l], w_e_up[l], b_e_up[l],
                  w_e_down[l], b_e_down[l])
        return layer_norm(DEEPNORM_ALPHA * x + ffn, ln2_g[l], ln2_b[l])

    y_prompt = x_prompt
    y_sample = x_sample
    for l in range(DEPTH):
        y_prompt = layer(y_prompt, l)
        y_sample = layer(y_sample, l)
    return (y_prompt, y_sample)
```

```python
import functools
import math

import numpy as np
import jax
import jax.numpy as jnp
from jax import lax
from jax.experimental import pallas as pl
from jax.experimental.pallas import tpu as pltpu

D_MODEL = 1024
DEPTH = 1
N_HEADS_ATTN = 8
HEAD_DIM_ATTN = 64
D_ATTN = N_HEADS_ATTN * HEAD_DIM_ATTN
DILATIONS = (1, 4, 16)
HALF_KEYS = 64
NUM_BUCKETS = 32
MAX_DISTANCE = 1024
N_HEADS_HGRN = 4
HEAD_DIM_HGRN = 128
D_HGRN = N_HEADS_HGRN * HEAD_DIM_HGRN
HGRN_CHUNK = 64
D_MIX = D_ATTN + D_HGRN
D_IN = 3 * D_ATTN + 5 * D_HGRN
N_EXPERTS = 32
TOP_K = 4
D_FF = 1024
SWIGLU_LIMIT = 7.0
SWIGLU_ALPHA = 1.702
DEEPNORM_ALPHA = (2.0 * DEPTH) ** 0.25
LN_EPS = 1e-5
NORM_EPS = 1e-6

LANES = 128
VMEM_LIMIT_BYTES = 56 * 2 ** 20

NEG_BIG = -1e30

TM_PROJ = 256
N_CHUNK_PROJ = 512
TQ_ATTN = 1024
TS_HGRN = 512
TM_OUT = 256
TM_MOE = 512
TM_COMB = 256

BF16 = jnp.bfloat16
F32 = jnp.float32


def _cparams(sem):
    return pltpu.CompilerParams(dimension_semantics=sem, vmem_limit_bytes=VMEM_LIMIT_BYTES)


def _inproj_body(x_ref, w_ref, o_ref):
    xb = x_ref[...].astype(BF16)
    for n in range(0, D_IN, N_CHUNK_PROJ):
        o_ref[:, n:n + N_CHUNK_PROJ] = jnp.dot(
            xb, w_ref[:, n:n + N_CHUNK_PROJ], preferred_element_type=F32)


def _inproj(x2d, w_in_bf16):
    T = x2d.shape[0]
    return pl.pallas_call(
        _inproj_body,
        grid=(T // TM_PROJ,),
        in_specs=[pl.BlockSpec((TM_PROJ, D_MODEL), lambda i: (i, 0)),
                  pl.BlockSpec((D_MODEL, D_IN), lambda i: (0, 0))],
        out_specs=pl.BlockSpec((TM_PROJ, D_IN), lambda i: (i, 0)),
        out_shape=jax.ShapeDtypeStruct((T, D_IN), F32),
        compiler_params=_cparams(("parallel",)),
        name="inproj",
    )(x2d, w_in_bf16)


def _t5_bucket_np(rel):
    half = NUM_BUCKETS // 2
    ret = np.where(rel > 0, half, 0)
    n = np.abs(rel)
    max_exact = half // 2
    nf = np.maximum(n, 1).astype(np.float32)
    large = max_exact + (np.log(nf / np.float32(max_exact)) / np.float32(math.log(MAX_DISTANCE / max_exact))
                         * np.float32(half - max_exact)).astype(np.int32)
    large = np.minimum(large, half - 1)
    return ret + np.where(n < max_exact, n, large)


def _attn_bias_tables(rel_bias):
    W = HALF_KEYS
    rel = np.arange(3 * W)[None, :] - W - np.arange(W)[:, None]
    band = np.abs(rel) <= W
    tabs = []
    for d in DILATIONS:
        b = jnp.transpose(rel_bias[_t5_bucket_np(rel * d)], (2, 0, 1)).astype(F32)
        tabs.append(jnp.where(band[None], b, NEG_BIG))
    return jnp.stack(tabs)


def _attn_body(q_ref, kp_ref, kc_ref, kn_ref, vp_ref, vc_ref, vn_ref, bias_ref, o_ref,
               qd, kd, vd, acc_p, m_p, den_p, acc_r, m_r, den_r, *, seq_len):
    W = HALF_KEYS
    TQ = TQ_ATTN
    i = pl.program_id(2)
    lane = lax.broadcasted_iota(jnp.int32, (1, LANES), 1)
    first_half = lane < HEAD_DIM_ATTN
    col = lax.broadcasted_iota(jnp.int32, (1, 3 * W), 1)

    for p, d in enumerate(DILATIONS):
        L = TQ // d
        nbp = L // W
        for r in range(d):
            qd[r * L:(r + 1) * L, :] = q_ref[pl.ds(r, L, stride=d), :].astype(BF16)
            for j, (kr, vr) in enumerate(((kp_ref, vp_ref), (kc_ref, vc_ref), (kn_ref, vn_ref))):
                kd[(3 * r + j) * L:(3 * r + j + 1) * L, :] = kr[pl.ds(r, L, stride=d), :].astype(BF16)
                vd[(3 * r + j) * L:(3 * r + j + 1) * L, :] = vr[pl.ds(r, L, stride=d), :].astype(BF16)

        sub_len = seq_len // d

        def unit(u, carry, p=p, L=L, nbp=nbp, sub_len=sub_len):
            r = u // nbp
            nb = u - r * nbp
            qoff = pl.multiple_of(u * W, W)
            koff = pl.multiple_of(r * (3 * L) + L - W + nb * W, W)
            q = qd[pl.ds(qoff, W), :]
            k = kd[pl.ds(koff, 3 * W), :]
            v = vd[pl.ds(koff, 3 * W), :]
            lo = i * L + nb * W - W
            kpos = lo + col
            colmask = jnp.where((kpos >= 0) & (kpos < sub_len), 0.0, NEG_BIG).astype(F32)
            pv, mm = [], []
            for hh in range(2):
                sel = first_half if hh == 0 else jnp.logical_not(first_half)
                qm = jnp.where(sel, q, jnp.zeros_like(q))
                s = lax.dot_general(qm, k, (((1,), (1,)), ((), ())), preferred_element_type=F32)
                s = s + bias_ref[p, hh] + colmask
                m = jnp.max(s, axis=-1, keepdims=True)
                e = jnp.exp(s - m).astype(BF16)
                vh = jnp.where(sel, v, jnp.ones_like(v))
                pv.append(jnp.dot(e, vh, preferred_element_type=F32))
                mm.append(m)
            acc_p[pl.ds(qoff, W), :] = jnp.where(first_half, pv[0], pv[1])
            den_sw = jnp.where(first_half, pv[1], pv[0])
            den_p[pl.ds(qoff, W), :] = pltpu.roll(den_sw, HEAD_DIM_ATTN, 1)
            m_p[pl.ds(qoff, W), :] = jnp.where(first_half, mm[0], mm[1])
            return carry

        lax.fori_loop(0, TQ // W, unit, 0)

        if p == 0:
            acc_r[...] = acc_p[...]
            m_r[...] = m_p[...]
            den_r[...] = den_p[...]
        else:
            for r in range(d):
                nat = pl.ds(r, L, stride=d)
                blk = slice(r * L, (r + 1) * L)
                m_old = m_r[nat, :]
                m_new_p = m_p[blk, :]
                m_new = jnp.maximum(m_old, m_new_p)
                a = jnp.exp(m_old - m_new)
                b = jnp.exp(m_new_p - m_new)
                acc_r[nat, :] = acc_r[nat, :] * a + acc_p[blk, :] * b
                den_r[nat, :] = den_r[nat, :] * a + den_p[blk, :] * b
                m_r[nat, :] = m_new

    o_ref[...] = (acc_r[...] / den_r[...]).astype(o_ref.dtype)


def _attention(h, bias_tabs, batch, seq_len):
    TQ = TQ_ATTN
    nT = seq_len // TQ
    npairs = N_HEADS_ATTN // 2
    qcol, kcol, vcol = 0, D_ATTN // LANES, 2 * D_ATTN // LANES

    def spec(col0, shift):
        def imap(b, j, i):
            ii = jnp.clip(i + shift, 0, nT - 1)
            return (b * nT + ii, col0 + j)
        return pl.BlockSpec((TQ, LANES), imap)

    in_specs = [spec(qcol, 0),
                spec(kcol, -1), spec(kcol, 0), spec(kcol, 1),
                spec(vcol, -1), spec(vcol, 0), spec(vcol, 1),
                pl.BlockSpec((len(DILATIONS), 2, HALF_KEYS, 3 * HALF_KEYS), lambda b, j, i: (0, j, 0, 0))]
    scratch = [pltpu.VMEM((TQ, LANES), BF16), pltpu.VMEM((3 * TQ, LANES), BF16), pltpu.VMEM((3 * TQ, LANES), BF16)]
    scratch += [pltpu.VMEM((TQ, LANES), F32) for _ in range(6)]
    return pl.pallas_call(
        functools.partial(_attn_body, seq_len=seq_len),
        grid=(batch, npairs, nT),
        in_specs=in_specs,
        out_specs=pl.BlockSpec((TQ, LANES), lambda b, j, i: (b * nT + i, j)),
        out_shape=jax.ShapeDtypeStruct((batch * seq_len, D_ATTN), BF16),
        scratch_shapes=scratch,
        compiler_params=_cparams(("parallel", "parallel", "arbitrary")),
        name="dilated_attn",
    )(h, h, h, h, h, h, h, bias_tabs)


def _hgrn_chunk(q, v, z, lb, st_ref, mask, mask_bf, last_row):
    f = lb + (1.0 - lb) * jax.nn.sigmoid(z)
    lf = jnp.log(f)
    kk = 1.0 - f
    hi = lf.astype(BF16)
    lo = (lf - hi.astype(F32)).astype(BF16)
    b = jnp.dot(mask_bf, hi, preferred_element_type=F32) + jnp.dot(mask_bf, lo, preferred_element_type=F32)
    b_last = b[last_row:last_row + 1, :]
    qd = (q * jnp.exp(b) * (HEAD_DIM_HGRN ** -0.5)).astype(BF16)
    kd = (kk * jnp.exp(-b)).astype(BF16)
    kl = (kk * jnp.exp(b_last - b)).astype(BF16)
    vb = v.astype(BF16)
    a = lax.dot_general(qd, kd, (((1,), (1,)), ((), ())), preferred_element_type=F32)
    a = jnp.where(mask, a, 0.0).astype(BF16)
    st = st_ref[...]
    o = jnp.dot(a, vb, preferred_element_type=F32)
    o = o + lax.dot_general(qd, st.astype(BF16), (((1,), (1,)), ((), ())), preferred_element_type=F32)
    upd = lax.dot_general(vb, kl, (((0,), (0,)), ((), ())), preferred_element_type=F32)
    st_ref[...] = st * jnp.exp(b_last) + upd
    return o


def _hgrn_body(qf_ref, vf_ref, zf_ref, qb_ref, vb_ref, zb_ref, lbf_ref, lbb_ref, of_ref, ob_ref, sf, sb):
    C = HGRN_CHUNK
    nch = TS_HGRN // C

    @pl.when(pl.program_id(2) == 0)
    def _():
        sf[...] = jnp.zeros_like(sf)
        sb[...] = jnp.zeros_like(sb)

    row = lax.broadcasted_iota(jnp.int32, (C, C), 0)
    colj = lax.broadcasted_iota(jnp.int32, (C, C), 1)
    tril = row >= colj
    triu = row <= colj
    tril_bf = jnp.where(tril, 1.0, 0.0).astype(BF16)
    triu_bf = jnp.where(triu, 1.0, 0.0).astype(BF16)
    lbf = lbf_ref[...]
    lbb = lbb_ref[...]
    for c in range(nch):
        rf = slice(c * C, (c + 1) * C)
        of_ref[rf, :] = _hgrn_chunk(qf_ref[rf, :], vf_ref[rf, :], zf_ref[rf, :], lbf, sf, tril, tril_bf, C - 1)
        cb = nch - 1 - c
        rb = slice(cb * C, (cb + 1) * C)
        ob_ref[rb, :] = _hgrn_chunk(qb_ref[rb, :], vb_ref[rb, :], zb_ref[rb, :], lbb, sb, triu, triu_bf, 0)


def _hgrn(h, lb_fwd, lb_bwd, batch, seq_len):
    TS = TS_HGRN
    nT = seq_len // TS
    c0 = 3 * D_ATTN // LANES
    qcol, zfcol, zbcol, vcol = c0, c0 + 4, c0 + 8, c0 + 12

    def fspec(col0):
        return pl.BlockSpec((TS, LANES), lambda b, hd, i: (b * nT + i, col0 + hd))

    def bspec(col0):
        return pl.BlockSpec((TS, LANES), lambda b, hd, i: (b * nT + nT - 1 - i, col0 + hd))

    lbspec = pl.BlockSpec((1, LANES), lambda b, hd, i: (0, hd))
    out_sds = jax.ShapeDtypeStruct((batch * seq_len, D_HGRN), F32)
    return pl.pallas_call(
        _hgrn_body,
        grid=(batch, N_HEADS_HGRN, nT),
        in_specs=[fspec(qcol), fspec(vcol), fspec(zfcol), bspec(qcol), bspec(vcol), bspec(zbcol), lbspec, lbspec],
        out_specs=[pl.BlockSpec((TS, LANES), lambda b, hd, i: (b * nT + i, hd)),
                   pl.BlockSpec((TS, LANES), lambda b, hd, i: (b * nT + nT - 1 - i, hd))],
        out_shape=[out_sds, out_sds],
        scratch_shapes=[pltpu.VMEM((HEAD_DIM_HGRN, HEAD_DIM_HGRN), F32) for _ in range(2)],
        compiler_params=_cparams(("parallel", "parallel", "arbitrary")),
        name="hgrn2",
    )(h, h, h, h, h, h, lb_fwd, lb_bwd)


def _layer_norm(y, g, b):
    mu = jnp.mean(y, axis=-1, keepdims=True)
    yc = y - mu
    var = jnp.mean(yc * yc, axis=-1, keepdims=True)
    return yc * lax.rsqrt(var + LN_EPS) * g + b


def _split_bf16(a):
    hi = a.astype(BF16)
    lo = (a - hi.astype(F32)).astype(BF16)
    return hi, lo


def _outproj_body(attn_ref, of_ref, ob_ref, g_ref, x_ref, wo_ref, ng_ref, l1g_ref, l1b_ref,
                  wrh_ref, wrl_ref, br_ref, x1_ref, idx_ref, gw_ref):
    o = of_ref[...] + ob_ref[...]
    parts = []
    for hd in range(N_HEADS_HGRN):
        oh = o[:, hd * HEAD_DIM_HGRN:(hd + 1) * HEAD_DIM_HGRN]
        ms = jnp.mean(oh * oh, axis=-1, keepdims=True)
        parts.append(oh * lax.rsqrt(ms + NORM_EPS))
    g = g_ref[...]
    hg = jnp.concatenate(parts, axis=-1) * ng_ref[...] * (g * jax.nn.sigmoid(g))
    mix = jnp.dot(attn_ref[...], wo_ref[:D_ATTN, :], preferred_element_type=F32)
    mix = mix + jnp.dot(hg.astype(BF16), wo_ref[D_ATTN:, :], preferred_element_type=F32)
    x1 = _layer_norm(DEEPNORM_ALPHA * x_ref[...] + mix, l1g_ref[...], l1b_ref[...])
    x1_ref[...] = x1

    xh, xl = _split_bf16(x1)
    logits = (jnp.dot(xh, wrh_ref[...], preferred_element_type=F32)
              + jnp.dot(xh, wrl_ref[...], preferred_element_type=F32)
              + jnp.dot(xl, wrh_ref[...], preferred_element_type=F32)) + br_ref[...]
    lane = lax.broadcasted_iota(jnp.int32, logits.shape, 1)
    vals, idxs = [], []
    l = logits
    for _ in range(TOP_K):
        m = jnp.max(l, axis=-1, keepdims=True)
        ik = jnp.min(jnp.where(l == m, lane, LANES), axis=-1, keepdims=True)
        vals.append(m)
        idxs.append(ik)
        l = jnp.where(lane == ik, -jnp.inf, l)
    es = [jnp.exp(v - vals[0]) for v in vals]
    den = es[0] + es[1] + es[2] + es[3]
    idx_out = jnp.zeros(logits.shape, jnp.int32)
    gw_out = jnp.zeros(logits.shape, F32)
    for k in range(TOP_K):
        idx_out = jnp.where(lane == k, idxs[k], idx_out)
        gw_out = jnp.where(lane == k, es[k] / den, gw_out)
    idx_ref[...] = idx_out
    gw_ref[...] = gw_out


def _outproj(attn, o_f, o_b, h, x2d, wo_bf16, ng_tiled, l1g, l1b, wr_hi, wr_lo, br_pad):
    T = x2d.shape[0]
    TM = TM_OUT
    gcol = (D_IN - D_HGRN) // D_HGRN
    row = lambda w: pl.BlockSpec((TM, w), lambda i: (i, 0))
    full = lambda a: pl.BlockSpec(a.shape, lambda i: (0,) * a.ndim)
    return pl.pallas_call(
        _outproj_body,
        grid=(T // TM,),
        in_specs=[row(D_ATTN), row(D_HGRN), row(D_HGRN),
                  pl.BlockSpec((TM, D_HGRN), lambda i: (i, gcol)),
                  row(D_MODEL), full(wo_bf16), full(ng_tiled), full(l1g), full(l1b),
                  full(wr_hi), full(wr_lo), full(br_pad)],
        out_specs=[row(D_MODEL), row(LANES), row(LANES)],
        out_shape=[jax.ShapeDtypeStruct((T, D_MODEL), F32),
                   jax.ShapeDtypeStruct((T, LANES), jnp.int32),
                   jax.ShapeDtypeStruct((T, LANES), F32)],
        compiler_params=_cparams(("parallel",)),
        name="outproj_ln_router",
    )(attn, o_f, o_b, h, x2d, wo_bf16, ng_tiled, l1g, l1b, wr_hi, wr_lo, br_pad)


def _moe_body(tile_e_ref, n_used_ref, tok_ref, x1_hbm, wg_ref, bg_ref, wu_ref, bu_ref, wd_ref, bd_ref,
              y_ref, xbuf, sem):
    t = pl.program_id(0)

    @pl.when(t < n_used_ref[0])
    def _():
        def issue(r, c):
            tok = tok_ref[0, 0, r]
            pltpu.make_async_copy(x1_hbm.at[pl.ds(tok, 1), :], xbuf.at[pl.ds(r, 1), :], sem).start()
            return c
        lax.fori_loop(0, TM_MOE, issue, 0)
        pltpu.make_async_copy(x1_hbm.at[pl.ds(0, TM_MOE), :], xbuf, sem).wait()
        xb = xbuf[...].astype(BF16)
        gt = jnp.minimum(jnp.dot(xb, wg_ref[0], preferred_element_type=F32) + bg_ref[0], SWIGLU_LIMIT)
        up = jnp.clip(jnp.dot(xb, wu_ref[0], preferred_element_type=F32) + bu_ref[0], -SWIGLU_LIMIT, SWIGLU_LIMIT)
        hdn = (up + 1.0) * gt * jax.nn.sigmoid(SWIGLU_ALPHA * gt)
        y_ref[...] = jnp.dot(hdn.astype(BF16), wd_ref[0], preferred_element_type=F32) + bd_ref[0]

    @pl.when(t >= n_used_ref[0])
    def _():
        y_ref[...] = jnp.zeros_like(y_ref)


def _moe_experts(x1, tile_e, n_used, tok_sorted, wg, bg, wu, bu, wd, bd):
    n_tiles = tile_e.shape[0]
    wspec = pl.BlockSpec((1, D_MODEL, D_FF), lambda t, te, nu: (te[t], 0, 0))
    bspec = pl.BlockSpec((1, 1, D_FF), lambda t, te, nu: (te[t], 0, 0))
    grid_spec = pltpu.PrefetchScalarGridSpec(
        num_scalar_prefetch=2,
        grid=(n_tiles,),
        in_specs=[pl.BlockSpec((1, 1, TM_MOE), lambda t, te, nu: (t, 0, 0), memory_space=pltpu.SMEM),
                  pl.BlockSpec(memory_space=pl.ANY),
                  wspec, bspec, wspec, bspec, wspec, bspec],
        out_specs=pl.BlockSpec((TM_MOE, D_MODEL), lambda t, te, nu: (t, 0)),
        scratch_shapes=[pltpu.VMEM((TM_MOE, D_MODEL), F32), pltpu.SemaphoreType.DMA],
    )
    return pl.pallas_call(
        _moe_body,
        grid_spec=grid_spec,
        out_shape=jax.ShapeDtypeStruct((n_tiles * TM_MOE, D_MODEL), F32),
        compiler_params=_cparams(("arbitrary",)),
        name="moe_experts",
    )(tile_e, n_used, tok_sorted.reshape(n_tiles, 1, TM_MOE), x1, wg, bg, wu, bu, wd, bd)


def _combine_body(pos_ref, x1_ref, gw_ref, l2g_ref, l2b_ref, y_hbm, o_ref, ybuf, sem):
    TM = TM_COMB

    def issue(r, c):
        for k in range(TOP_K):
            pos = pos_ref[0, 0, k * TM + r]
            pltpu.make_async_copy(y_hbm.at[pl.ds(pos, 1), :], ybuf.at[k, pl.ds(r, 1), :], sem).start()
        return c
    lax.fori_loop(0, TM, issue, 0)
    for k in range(TOP_K):
        pltpu.make_async_copy(y_hbm.at[pl.ds(0, TM), :], ybuf.at[k], sem).wait()
    gw = gw_ref[...]
    ffn = gw[:, 0:1] * ybuf[0]
    for k in range(1, TOP_K):
        ffn = ffn + gw[:, k:k + 1] * ybuf[k]
    o_ref[...] = _layer_norm(DEEPNORM_ALPHA * x1_ref[...] + ffn, l2g_ref[...], l2b_ref[...])


def _combine(pos_tiles, x1, gw, l2g, l2b, y_sorted):
    T = x1.shape[0]
    TM = TM_COMB
    full = lambda a: pl.BlockSpec(a.shape, lambda i: (0,) * a.ndim)
    return pl.pallas_call(
        _combine_body,
        grid=(T // TM,),
        in_specs=[pl.BlockSpec((1, 1, TOP_K * TM), lambda i: (i, 0, 0), memory_space=pltpu.SMEM),
                  pl.BlockSpec((TM, D_MODEL), lambda i: (i, 0)),
                  pl.BlockSpec((TM, LANES), lambda i: (i, 0)),
                  full(l2g), full(l2b),
                  pl.BlockSpec(memory_space=pl.ANY)],
        out_specs=pl.BlockSpec((TM, D_MODEL), lambda i: (i, 0)),
        out_shape=jax.ShapeDtypeStruct((T, D_MODEL), F32),
        scratch_shapes=[pltpu.VMEM((TOP_K, TM, D_MODEL), F32), pltpu.SemaphoreType.DMA],
        compiler_params=_cparams(("arbitrary",)),
        name="combine_ln",
    )(pos_tiles, x1, gw, l2g, l2b, y_sorted)


def _route(top_idx, n_tiles):
    T = top_idx.shape[0]
    n_slots = T * TOP_K
    e = top_idx.reshape(n_slots)
    onehot = (e[:, None] == jnp.arange(N_EXPERTS, dtype=jnp.int32)[None, :]).astype(jnp.int32)
    csum = jnp.cumsum(onehot, axis=0)
    rank = jnp.sum(csum * onehot, axis=1) - 1
    counts = csum[-1]
    tiles_e = (counts + TM_MOE - 1) // TM_MOE
    tile_end = jnp.cumsum(tiles_e)
    tile_start = tile_end - tiles_e
    pos = jnp.sum(onehot * (tile_start * TM_MOE)[None, :], axis=1) + rank
    n_used = tile_end[-1:]
    tile_ids = jnp.arange(n_tiles, dtype=jnp.int32)
    tile_e = jnp.sum((tile_ids[:, None] >= tile_end[None, :]).astype(jnp.int32), axis=1)
    tile_e = jnp.minimum(tile_e, N_EXPERTS - 1)
    tok_sorted = jnp.zeros((n_tiles * TM_MOE,), jnp.int32).at[pos].set(
        jnp.arange(n_slots, dtype=jnp.int32) // TOP_K, unique_indices=True)
    return pos, tok_sorted, tile_e.astype(jnp.int32), n_used.astype(jnp.int32)


def _moe_ln(x1, top_idx, gw, wg, bg, wu, bu, wd, bd, l2g, l2b):
    T = x1.shape[0]
    n_tiles = T * TOP_K // TM_MOE + N_EXPERTS
    pos, tok_sorted, tile_e, n_used = _route(top_idx, n_tiles)
    y_sorted = _moe_experts(x1, tile_e, n_used, tok_sorted, wg, bg, wu, bu, wd, bd)
    pos_tiles = pos.reshape(T // TM_COMB, TM_COMB, TOP_K).transpose(0, 2, 1).reshape(T // TM_COMB, 1, TOP_K * TM_COMB)
    return _combine(pos_tiles, x1, gw, l2g, l2b, y_sorted)


def _mixer_ln_router(x, w_in_bf16, wo_bf16, bias_tabs, lb_f, lb_b, ng_tiled, l1g, l1b, wr_hi, wr_lo, br_pad):
    batch, seq_len, _ = x.shape
    x2d = x.reshape(batch * seq_len, D_MODEL)
    h = _inproj(x2d, w_in_bf16)
    attn = _attention(h, bias_tabs, batch, seq_len)
    o_f, o_b = _hgrn(h, lb_f, lb_b, batch, seq_len)
    return _outproj(attn, o_f, o_b, h, x2d, wo_bf16, ng_tiled, l1g, l1b, wr_hi, wr_lo, br_pad)


def kernel(x_prompt, x_sample, w_in, w_out, rel_bias, hgrn_lb_fwd, hgrn_lb_bwd, hgrn_norm_g, ln1_g, ln1_b,
           w_router, b_router, w_e_gate, b_e_gate, w_e_up, b_e_up, w_e_down, b_e_down, ln2_g, ln2_b):
    assert DEPTH == 1
    l = 0
    lb_f = jnp.cumsum(jax.nn.softmax(hgrn_lb_fwd.astype(F32), axis=0), axis=0)[l].reshape(1, D_HGRN)
    lb_b = jnp.cumsum(jax.nn.softmax(hgrn_lb_bwd.astype(F32), axis=0), axis=0)[l].reshape(1, D_HGRN)
    col_scale = jnp.concatenate([jnp.full((D_ATTN,), HEAD_DIM_ATTN ** -0.5, F32), jnp.ones((D_IN - D_ATTN,), F32)])
    w_in_bf16 = (w_in[l] * col_scale).astype(BF16)
    wo_bf16 = w_out[l].astype(BF16)
    bias_tabs = _attn_bias_tables(rel_bias)
    ng_tiled = jnp.tile(hgrn_norm_g[l].astype(F32), N_HEADS_HGRN).reshape(1, D_HGRN)
    row = lambda a: a.astype(F32).reshape(1, -1)
    wr = jnp.pad(w_router[l].astype(F32), ((0, 0), (0, LANES - N_EXPERTS)))
    wr_hi, wr_lo = _split_bf16(wr)
    br_pad = jnp.pad(row(b_router[l]), ((0, 0), (0, LANES - N_EXPERTS)), constant_values=NEG_BIG)

    outs = [_mixer_ln_router(x, w_in_bf16, wo_bf16, bias_tabs, lb_f, lb_b, ng_tiled, row(ln1_g[l]), row(ln1_b[l]),
                             wr_hi, wr_lo, br_pad) for x in (x_prompt, x_sample)]
    x1 = jnp.concatenate([o[0] for o in outs], axis=0)
    top_idx = jnp.concatenate([o[1][:, :TOP_K] for o in outs], axis=0)
    gw = jnp.concatenate([o[2] for o in outs], axis=0)

    y = _moe_ln(x1, top_idx, gw,
                w_e_gate[l].astype(BF16), b_e_gate[l].astype(F32).reshape(N_EXPERTS, 1, D_FF),
                w_e_up[l].astype(BF16), b_e_up[l].astype(F32).reshape(N_EXPERTS, 1, D_FF),
                w_e_down[l].astype(BF16), b_e_down[l].astype(F32).reshape(N_EXPERTS, 1, D_MODEL),
                row(ln2_g[l]), row(ln2_b[l]))
    n_p = x_prompt.shape[0] * x_prompt.shape[1]
    return (y[:n_p].reshape(x_prompt.shape), y[n_p:].reshape(x_sample.shape))
```

```python
import functools
import math

import numpy as np
import jax
import jax.numpy as jnp
from jax import lax
from jax.experimental import pallas as pl
from jax.experimental.pallas import tpu as pltpu

D_MODEL = 1024
DEPTH = 1
N_HEADS_ATTN = 8
HEAD_DIM_ATTN = 64
D_ATTN = N_HEADS_ATTN * HEAD_DIM_ATTN
DILATIONS = (1, 4, 16)
HALF_KEYS = 64
NUM_BUCKETS = 32
MAX_DISTANCE = 1024
N_HEADS_HGRN = 4
HEAD_DIM_HGRN = 128
D_HGRN = N_HEADS_HGRN * HEAD_DIM_HGRN
HGRN_CHUNK = 64
D_MIX = D_ATTN + D_HGRN
D_IN = 3 * D_ATTN + 5 * D_HGRN
N_EXPERTS = 32
TOP_K = 4
D_FF = 1024
SWIGLU_LIMIT = 7.0
SWIGLU_ALPHA = 1.702
DEEPNORM_ALPHA = (2.0 * DEPTH) ** 0.25
LN_EPS = 1e-5
NORM_EPS = 1e-6

LANES = 128
VMEM_LIMIT_BYTES = 56 * 2 ** 20

NEG_BIG = -1e30

RES = DILATIONS[-1]
TQ_ATTN = 2048
L_ATTN = TQ_ATTN // RES
UQ = 2 * HALF_KEYS
UK = 4 * HALF_KEYS
N_UNITS = TQ_ATTN // UQ
TM_PROJ = 256
L_PROJ = TM_PROJ // RES
TS_HGRN = 512
TM_OUT = 256
TM_MOE = 512
TM_COMB = 256

BF16 = jnp.bfloat16
F32 = jnp.float32


def _cparams(sem):
    return pltpu.CompilerParams(dimension_semantics=sem, vmem_limit_bytes=VMEM_LIMIT_BYTES)


def _inproj_body(x_ref, w_ref, q16_ref, kv16_ref, kvn_ref, hh_ref, xs):
    xb = x_ref[...].astype(BF16)
    nchunk = D_MODEL // LANES
    for c in range(nchunk):
        xs[c] = x_ref[:, c * LANES:(c + 1) * LANES]
    xp = jnp.concatenate(
        [jnp.concatenate([xs[c, pl.ds(r, L_PROJ, stride=RES), :] for r in range(RES)], axis=0)
         for c in range(nchunk)], axis=1).astype(BF16)
    kv0, hh0 = D_ATTN, 3 * D_ATTN
    kvn_ref[...] = jnp.dot(xb, w_ref[:, kv0:hh0], preferred_element_type=F32).astype(BF16)
    for n in range(hh0, D_IN, D_HGRN):
        hh_ref[:, n - hh0:n - hh0 + D_HGRN] = jnp.dot(xb, w_ref[:, n:n + D_HGRN], preferred_element_type=F32)
    qp = jnp.dot(xp, w_ref[:, :kv0], preferred_element_type=F32)
    kvp = jnp.dot(xp, w_ref[:, kv0:hh0], preferred_element_type=F32).astype(BF16)
    for r in range(RES):
        q16_ref[0, r] = qp[r * L_PROJ:(r + 1) * L_PROJ, :]
        kv16_ref[0, r] = kvp[r * L_PROJ:(r + 1) * L_PROJ, :]


def _inproj(x2d, w_in_bf16):
    T = x2d.shape[0]
    n_tiles = T // TQ_ATTN
    steps_per_tile = TQ_ATTN // TM_PROJ
    perm_map = lambda i: (i // steps_per_tile, 0, i % steps_per_tile, 0)
    return pl.pallas_call(
        _inproj_body,
        grid=(T // TM_PROJ,),
        in_specs=[pl.BlockSpec((TM_PROJ, D_MODEL), lambda i: (i, 0)),
                  pl.BlockSpec((D_MODEL, D_IN), lambda i: (0, 0))],
        out_specs=[pl.BlockSpec((1, RES, L_PROJ, D_ATTN), perm_map),
                   pl.BlockSpec((1, RES, L_PROJ, 2 * D_ATTN), perm_map),
                   pl.BlockSpec((TM_PROJ, 2 * D_ATTN), lambda i: (i, 0)),
                   pl.BlockSpec((TM_PROJ, 5 * D_HGRN), lambda i: (i, 0))],
        out_shape=[jax.ShapeDtypeStruct((n_tiles, RES, L_ATTN, D_ATTN), F32),
                   jax.ShapeDtypeStruct((n_tiles, RES, L_ATTN, 2 * D_ATTN), BF16),
                   jax.ShapeDtypeStruct((T, 2 * D_ATTN), BF16),
                   jax.ShapeDtypeStruct((T, 5 * D_HGRN), F32)],
        scratch_shapes=[pltpu.VMEM((D_MODEL // LANES, TM_PROJ, LANES), F32)],
        compiler_params=_cparams(("parallel",)),
        name="inproj",
    )(x2d, w_in_bf16)


def _t5_bucket_np(rel):
    half = NUM_BUCKETS // 2
    ret = np.where(rel > 0, half, 0)
    n = np.abs(rel)
    max_exact = half // 2
    nf = np.maximum(n, 1).astype(np.float32)
    large = max_exact + (np.log(nf / np.float32(max_exact)) / np.float32(math.log(MAX_DISTANCE / max_exact))
                         * np.float32(half - max_exact)).astype(np.int32)
    large = np.minimum(large, half - 1)
    return ret + np.where(n < max_exact, n, large)


def _unit_geometry():
    W = HALF_KEYS
    rows = np.arange(UQ)
    cols = np.arange(UK)
    geo = []
    qo = RES * (rows % (UQ // RES)) + rows // (UQ // RES)
    geo.append(((cols[None, :] - W) - qo[:, None], cols < W, cols >= UK - W))
    nq, nk = UQ // 4, UK // 4
    aq, lq = rows // nq, rows % nq
    ak, lk = cols // nk, cols % nk
    rel4 = 4 * (lk[None, :] - W // 4 - lq[:, None]) + (ak[None, :] - aq[:, None])
    geo.append((rel4, lk < W // 4, lk >= nk - W // 4))
    geo.append(((cols[None, :] - W) - rows[:, None], cols < W, cols >= UK - W))
    return geo


def _attn_bias_tables(rel_bias):
    W = HALF_KEYS
    per_dil = []
    for d, (rel, prev_cols, next_cols) in zip(DILATIONS, _unit_geometry()):
        b = jnp.transpose(rel_bias[_t5_bucket_np(rel * d)], (2, 0, 1)).astype(F32)
        band = np.abs(rel) <= W
        variants = [band, band & ~prev_cols[None, :], band & ~next_cols[None, :]]
        per_dil.append(jnp.stack([jnp.where(v[None], b, NEG_BIG) for v in variants], axis=1))
    t = jnp.stack(per_dil, axis=1)
    t = t.reshape(N_HEADS_ATTN // 2, 2, len(DILATIONS), 3, UQ, UK)
    return jnp.transpose(t, (0, 2, 3, 1, 4, 5)).reshape(N_HEADS_ATTN // 2, len(DILATIONS), 3, 2 * UQ, UK)


def _attn_body(q16_ref, kc_ref, kp_ref, kx_ref, vc_ref, vp_ref, vx_ref,
               knc_ref, knp_ref, knx_ref, vnc_ref, vnp_ref, vnx_ref, tab_ref, o_ref,
               kd1, vd1, s_scr, e_scr, acc_p, m_p, den_p, acc_r, m_r, den_r, nat):
    W = HALF_KEYS
    i = pl.program_id(2)
    first = jnp.where(i == 0, 1, 0)
    last = jnp.where(i == pl.num_programs(2) - 1, 2, 0)
    lane = lax.broadcasted_iota(jnp.int32, (1, LANES), 1)
    first_half = lane < HEAD_DIM_ATTN
    ones_v = jnp.ones((UK, LANES), BF16)

    def scores(u, q, k, table):
        qb = q.astype(BF16)
        zero = jnp.zeros_like(qb)
        lhs = jnp.concatenate([jnp.where(first_half, qb, zero), jnp.where(first_half, zero, qb)], axis=0)
        s_scr[u] = lax.dot_general(lhs, k, (((1,), (1,)), ((), ())), preferred_element_type=F32) + table

    def softmax(u):
        s = s_scr[u]
        m = jnp.max(s, axis=-1, keepdims=True)
        e_scr[u] = jnp.exp(s - m).astype(BF16)
        return jnp.where(first_half, m[:UQ], m[UQ:])

    def weighted(u, v):
        r = jnp.dot(e_scr[u], jnp.concatenate([v, ones_v], axis=1), preferred_element_type=F32)
        acc = jnp.where(first_half, r[:UQ, :LANES], r[UQ:, :LANES])
        den = jnp.where(first_half, r[:UQ, LANES:], r[UQ:, LANES:])
        return acc, den

    def merge(p):
        if p == 0:
            acc_r[...] = acc_p[...]
            m_r[...] = m_p[...]
            den_r[...] = den_p[...]
        else:
            m_old = m_r[...]
            m_new_p = m_p[...]
            m_new = jnp.maximum(m_old, m_new_p)
            a = jnp.exp(m_old - m_new)
            b = jnp.exp(m_new_p - m_new)
            acc_r[...] = acc_r[...] * a + acc_p[...] * b
            den_r[...] = den_r[...] * a + den_p[...] * b
            m_r[...] = m_new

    kd1[0:W, :] = knp_ref[...]
    kd1[W:W + TQ_ATTN, :] = knc_ref[...]
    kd1[W + TQ_ATTN:, :] = knx_ref[...]
    vd1[0:W, :] = vnp_ref[...]
    vd1[W:W + TQ_ATTN, :] = vnc_ref[...]
    vd1[W + TQ_ATTN:, :] = vnx_ref[...]
    nl = UQ // RES

    def a1(u, c):
        lo = pl.multiple_of(u * nl, nl)
        q = jnp.concatenate([q16_ref[0, r, pl.ds(lo, nl), :] for r in range(RES)], axis=0)
        k = kd1[pl.ds(pl.multiple_of(u * UQ, UQ), UK), :]
        var = jnp.where(u == 0, first, 0) + jnp.where(u == N_UNITS - 1, last, 0)
        scores(u, q, k, tab_ref[0, 0, var])
        return c

    def b1(u, c):
        m = softmax(u)
        lo = pl.multiple_of(u * nl, nl)
        for r in range(RES):
            m_p[pl.ds(r * L_ATTN + lo, nl), :] = m[r * nl:(r + 1) * nl]
        return c

    def c1(u, c):
        acc, den = weighted(u, vd1[pl.ds(pl.multiple_of(u * UQ, UQ), UK), :])
        lo = pl.multiple_of(u * nl, nl)
        for r in range(RES):
            acc_p[pl.ds(r * L_ATTN + lo, nl), :] = acc[r * nl:(r + 1) * nl]
            den_p[pl.ds(r * L_ATTN + lo, nl), :] = den[r * nl:(r + 1) * nl]
        return c

    lax.fori_loop(0, N_UNITS, a1, 0, unroll=2)
    lax.fori_loop(0, N_UNITS, b1, 0, unroll=2)
    lax.fori_loop(0, N_UNITS, c1, 0, unroll=2)
    merge(0)

    nq, nk = UQ // 4, UK // 4
    nb4 = L_ATTN // nq

    def window4(cur, prev, nxt, r16, nb):
        lo = nb * nq - W // 4
        if lo < 0:
            return [prev[0, r16, W + lo:W, :], cur[0, r16, 0:lo + nk, :]]
        if lo + nk > L_ATTN:
            return [cur[0, r16, lo:L_ATTN, :], nxt[0, r16, 0:lo + nk - L_ATTN, :]]
        return [cur[0, r16, lo:lo + nk, :]]

    def a4(r4, c):
        for nb in range(nb4):
            q = jnp.concatenate([q16_ref[0, r4 + 4 * a, nb * nq:(nb + 1) * nq, :] for a in range(4)], axis=0)
            k = jnp.concatenate(sum([window4(kc_ref, kp_ref, kx_ref, r4 + 4 * a, nb) for a in range(4)], []), axis=0)
            var = first if nb == 0 else (last if nb == nb4 - 1 else 0)
            scores(r4 * nb4 + nb, q, k, tab_ref[0, 1, var])
        return c

    def b4(r4, c):
        for nb in range(nb4):
            m = softmax(r4 * nb4 + nb)
            for a in range(4):
                row0 = pl.multiple_of((r4 + 4 * a) * L_ATTN + nb * nq, nq)
                m_p[pl.ds(row0, nq), :] = m[a * nq:(a + 1) * nq]
        return c

    def c4(r4, c):
        for nb in range(nb4):
            v = jnp.concatenate(sum([window4(vc_ref, vp_ref, vx_ref, r4 + 4 * a, nb) for a in range(4)], []), axis=0)
            acc, den = weighted(r4 * nb4 + nb, v)
            for a in range(4):
                row0 = pl.multiple_of((r4 + 4 * a) * L_ATTN + nb * nq, nq)
                acc_p[pl.ds(row0, nq), :] = acc[a * nq:(a + 1) * nq]
                den_p[pl.ds(row0, nq), :] = den[a * nq:(a + 1) * nq]
        return c

    lax.fori_loop(0, 4, a4, 0)
    lax.fori_loop(0, 4, b4, 0)
    lax.fori_loop(0, 4, c4, 0)
    merge(1)

    var16 = first + last

    def a16(r, c):
        k = jnp.concatenate([kp_ref[0, r], kc_ref[0, r], kx_ref[0, r]], axis=0)
        scores(r, q16_ref[0, r], k, tab_ref[0, 2, var16])
        return c

    def b16(r, c):
        m_p[pl.ds(pl.multiple_of(r * L_ATTN, L_ATTN), L_ATTN), :] = softmax(r)
        return c

    def c16(r, c):
        v = jnp.concatenate([vp_ref[0, r], vc_ref[0, r], vx_ref[0, r]], axis=0)
        acc, den = weighted(r, v)
        rows = pl.ds(pl.multiple_of(r * L_ATTN, L_ATTN), L_ATTN)
        acc_p[rows, :] = acc
        den_p[rows, :] = den
        return c

    lax.fori_loop(0, RES, a16, 0, unroll=2)
    lax.fori_loop(0, RES, b16, 0, unroll=2)
    lax.fori_loop(0, RES, c16, 0, unroll=2)
    merge(2)

    for r in range(RES):
        rows = slice(r * L_ATTN, (r + 1) * L_ATTN)
        nat[pl.ds(r, L_ATTN, stride=RES), :] = acc_r[rows, :] / den_r[rows, :]
    o_ref[...] = nat[...].astype(o_ref.dtype)


def _attention(q16, kv16, kvn, bias_tabs, batch, seq_len):
    TQ, W = TQ_ATTN, HALF_KEYS
    nT = seq_len // TQ
    npairs = N_HEADS_ATTN // 2
    halo_per_tile = TQ // W
    vcol = D_ATTN // LANES

    def tile16(col0, shift, lblock, lsize):
        def imap(b, j, i):
            return (b * nT + jnp.clip(i + shift, 0, nT - 1), 0, lblock, col0 + j)
        return pl.BlockSpec((1, RES, lsize, LANES), imap)

    def nat_cur(col0):
        return pl.BlockSpec((TQ, LANES), lambda b, j, i: (b * nT + i, col0 + j))

    def nat_halo(col0, after):
        def imap(b, j, i):
            blk = (b * nT + i + after) * halo_per_tile - (1 - after)
            return (jnp.clip(blk, b * nT * halo_per_tile, (b + 1) * nT * halo_per_tile - 1), col0 + j)
        return pl.BlockSpec((W, LANES), imap)

    in_specs = [tile16(0, 0, 0, L_ATTN),
                tile16(0, 0, 0, L_ATTN), tile16(0, -1, L_ATTN // W - 1, W), tile16(0, 1, 0, W),
                tile16(vcol, 0, 0, L_ATTN), tile16(vcol, -1, L_ATTN // W - 1, W), tile16(vcol, 1, 0, W),
                nat_cur(0), nat_halo(0, 0), nat_halo(0, 1),
                nat_cur(vcol), nat_halo(vcol, 0), nat_halo(vcol, 1),
                pl.BlockSpec((1, len(DILATIONS), 3, 2 * UQ, UK), lambda b, j, i: (j, 0, 0, 0, 0))]
    scratch = [pltpu.VMEM((TQ + 2 * W, LANES), BF16), pltpu.VMEM((TQ + 2 * W, LANES), BF16),
               pltpu.VMEM((N_UNITS, 2 * UQ, UK), F32), pltpu.VMEM((N_UNITS, 2 * UQ, UK), BF16)]
    scratch += [pltpu.VMEM((TQ, LANES), F32) for _ in range(7)]
    return pl.pallas_call(
        _attn_body,
        grid=(batch, npairs, nT),
        in_specs=in_specs,
        out_specs=pl.BlockSpec((TQ, LANES), lambda b, j, i: (b * nT + i, j)),
        out_shape=jax.ShapeDtypeStruct((batch * seq_len, D_ATTN), BF16),
        scratch_shapes=scratch,
        compiler_params=_cparams(("parallel", "parallel", "arbitrary")),
        name="dilated_attn",
    )(q16, kv16, kv16, kv16, kv16, kv16, kv16, kvn, kvn, kvn, kvn, kvn, kvn, bias_tabs)


def _hgrn_chunk(q, v, z, lb, st_ref, mask, mask_bf, last_row):
    f = lb + (1.0 - lb) * jax.nn.sigmoid(z)
    lf = jnp.log(f)
    kk = 1.0 - f
    hi = lf.astype(BF16)
    lo = (lf - hi.astype(F32)).astype(BF16)
    b = jnp.dot(mask_bf, hi, preferred_element_type=F32) + jnp.dot(mask_bf, lo, preferred_element_type=F32)
    b_last = b[last_row:last_row + 1, :]
    qd = (q * jnp.exp(b) * (HEAD_DIM_HGRN ** -0.5)).astype(BF16)
    kd = (kk * jnp.exp(-b)).astype(BF16)
    kl = (kk * jnp.exp(b_last - b)).astype(BF16)
    vb = v.astype(BF16)
    a = lax.dot_general(qd, kd, (((1,), (1,)), ((), ())), preferred_element_type=F32)
    a = jnp.where(mask, a, 0.0).astype(BF16)
    st = st_ref[...]
    o = jnp.dot(a, vb, preferred_element_type=F32)
    o = o + lax.dot_general(qd, st.astype(BF16), (((1,), (1,)), ((), ())), preferred_element_type=F32)
    upd = lax.dot_general(vb, kl, (((0,), (0,)), ((), ())), preferred_element_type=F32)
    st_ref[...] = st * jnp.exp(b_last) + upd
    return o


def _hgrn_body(qf_ref, vf_ref, zf_ref, qb_ref, vb_ref, zb_ref, lbf_ref, lbb_ref, of_ref, ob_ref, sf, sb):
    C = HGRN_CHUNK
    nch = TS_HGRN // C

    @pl.when(pl.program_id(2) == 0)
    def _():
        sf[...] = jnp.zeros_like(sf)
        sb[...] = jnp.zeros_like(sb)

    row = lax.broadcasted_iota(jnp.int32, (C, C), 0)
    colj = lax.broadcasted_iota(jnp.int32, (C, C), 1)
    tril = row >= colj
    triu = row <= colj
    tril_bf = jnp.where(tril, 1.0, 0.0).astype(BF16)
    triu_bf = jnp.where(triu, 1.0, 0.0).astype(BF16)
    lbf = lbf_ref[...]
    lbb = lbb_ref[...]
    for c in range(nch):
        rf = slice(c * C, (c + 1) * C)
        of_ref[rf, :] = _hgrn_chunk(qf_ref[rf, :], vf_ref[rf, :], zf_ref[rf, :], lbf, sf, tril, tril_bf, C - 1)
        cb = nch - 1 - c
        rb = slice(cb * C, (cb + 1) * C)
        ob_ref[rb, :] = _hgrn_chunk(qb_ref[rb, :], vb_ref[rb, :], zb_ref[rb, :], lbb, sb, triu, triu_bf, 0)


def _hgrn(hh, lb_fwd, lb_bwd, batch, seq_len):
    TS = TS_HGRN
    nT = seq_len // TS
    qcol, zfcol, zbcol, vcol = 0, 4, 8, 12

    def fspec(col0):
        return pl.BlockSpec((TS, LANES), lambda b, hd, i: (b * nT + i, col0 + hd))

    def bspec(col0):
        return pl.BlockSpec((TS, LANES), lambda b, hd, i: (b * nT + nT - 1 - i, col0 + hd))

    lbspec = pl.BlockSpec((1, LANES), lambda b, hd, i: (0, hd))
    out_sds = jax.ShapeDtypeStruct((batch * seq_len, D_HGRN), F32)
    return pl.pallas_call(
        _hgrn_body,
        grid=(batch, N_HEADS_HGRN, nT),
        in_specs=[fspec(qcol), fspec(vcol), fspec(zfcol), bspec(qcol), bspec(vcol), bspec(zbcol), lbspec, lbspec],
        out_specs=[pl.BlockSpec((TS, LANES), lambda b, hd, i: (b * nT + i, hd)),
                   pl.BlockSpec((TS, LANES), lambda b, hd, i: (b * nT + nT - 1 - i, hd))],
        out_shape=[out_sds, out_sds],
        scratch_shapes=[pltpu.VMEM((HEAD_DIM_HGRN, HEAD_DIM_HGRN), F32) for _ in range(2)],
        compiler_params=_cparams(("parallel", "parallel", "arbitrary")),
        name="hgrn2",
    )(hh, hh, hh, hh, hh, hh, lb_fwd, lb_bwd)


def _layer_norm(y, g, b):
    mu = jnp.mean(y, axis=-1, keepdims=True)
    yc = y - mu
    var = jnp.mean(yc * yc, axis=-1, keepdims=True)
    return yc * lax.rsqrt(var + LN_EPS) * g + b


def _split_bf16(a):
    hi = a.astype(BF16)
    lo = (a - hi.astype(F32)).astype(BF16)
    return hi, lo


def _outproj_body(attn_ref, of_ref, ob_ref, g_ref, x_ref, wo_ref, ng_ref, l1g_ref, l1b_ref,
                  wrh_ref, wrl_ref, br_ref, x1_ref, idx_ref, gw_ref):
    o = of_ref[...] + ob_ref[...]
    parts = []
    for hd in range(N_HEADS_HGRN):
        oh = o[:, hd * HEAD_DIM_HGRN:(hd + 1) * HEAD_DIM_HGRN]
        ms = jnp.mean(oh * oh, axis=-1, keepdims=True)
        parts.append(oh * lax.rsqrt(ms + NORM_EPS))
    g = g_ref[...]
    hg = jnp.concatenate(parts, axis=-1) * ng_ref[...] * (g * jax.nn.sigmoid(g))
    mix = jnp.dot(attn_ref[...], wo_ref[:D_ATTN, :], preferred_element_type=F32)
    mix = mix + jnp.dot(hg.astype(BF16), wo_ref[D_ATTN:, :], preferred_element_type=F32)
    x1 = _layer_norm(DEEPNORM_ALPHA * x_ref[...] + mix, l1g_ref[...], l1b_ref[...])
    x1_ref[...] = x1

    xh, xl = _split_bf16(x1)
    logits = (jnp.dot(xh, wrh_ref[...], preferred_element_type=F32)
              + jnp.dot(xh, wrl_ref[...], preferred_element_type=F32)
              + jnp.dot(xl, wrh_ref[...], preferred_element_type=F32)) + br_ref[...]
    lane = lax.broadcasted_iota(jnp.int32, logits.shape, 1)
    vals, idxs = [], []
    l = logits
    for _ in range(TOP_K):
        m = jnp.max(l, axis=-1, keepdims=True)
        ik = jnp.min(jnp.where(l == m, lane, LANES), axis=-1, keepdims=True)
        vals.append(m)
        idxs.append(ik)
        l = jnp.where(lane == ik, -jnp.inf, l)
    es = [jnp.exp(v - vals[0]) for v in vals]
    den = es[0] + es[1] + es[2] + es[3]
    idx_out = jnp.zeros(logits.shape, jnp.int32)
    gw_out = jnp.zeros(logits.shape, F32)
    for k in range(TOP_K):
        idx_out = jnp.where(lane == k, idxs[k], idx_out)
        gw_out = jnp.where(lane == k, es[k] / den, gw_out)
    idx_ref[...] = idx_out
    gw_ref[...] = gw_out


def _outproj(attn, o_f, o_b, hh, x2d, wo_bf16, ng_tiled, l1g, l1b, wr_hi, wr_lo, br_pad):
    T = x2d.shape[0]
    TM = TM_OUT
    gcol = 4
    row = lambda w: pl.BlockSpec((TM, w), lambda i: (i, 0))
    full = lambda a: pl.BlockSpec(a.shape, lambda i: (0,) * a.ndim)
    return pl.pallas_call(
        _outproj_body,
        grid=(T // TM,),
        in_specs=[row(D_ATTN), row(D_HGRN), row(D_HGRN),
                  pl.BlockSpec((TM, D_HGRN), lambda i: (i, gcol)),
                  row(D_MODEL), full(wo_bf16), full(ng_tiled), full(l1g), full(l1b),
                  full(wr_hi), full(wr_lo), full(br_pad)],
        out_specs=[row(D_MODEL), row(LANES), row(LANES)],
        out_shape=[jax.ShapeDtypeStruct((T, D_MODEL), F32),
                   jax.ShapeDtypeStruct((T, LANES), jnp.int32),
                   jax.ShapeDtypeStruct((T, LANES), F32)],
        compiler_params=_cparams(("parallel",)),
        name="outproj_ln_router",
    )(attn, o_f, o_b, hh, x2d, wo_bf16, ng_tiled, l1g, l1b, wr_hi, wr_lo, br_pad)


def _moe_body(tile_e_ref, n_used_ref, tok_ref, x1_hbm, wg_ref, bg_ref, wu_ref, bu_ref, wd_ref, bd_ref,
              y_ref, xbuf, sem):
    t = pl.program_id(0)

    @pl.when(t < n_used_ref[0])
    def _():
        def issue(r, c):
            tok = tok_ref[0, 0, r]
            pltpu.make_async_copy(x1_hbm.at[pl.ds(tok, 1), :], xbuf.at[pl.ds(r, 1), :], sem).start()
            return c
        lax.fori_loop(0, TM_MOE, issue, 0)
        pltpu.make_async_copy(x1_hbm.at[pl.ds(0, TM_MOE), :], xbuf, sem).wait()
        xb = xbuf[...].astype(BF16)
        gt = jnp.minimum(jnp.dot(xb, wg_ref[0], preferred_element_type=F32) + bg_ref[0], SWIGLU_LIMIT)
        up = jnp.clip(jnp.dot(xb, wu_ref[0], preferred_element_type=F32) + bu_ref[0], -SWIGLU_LIMIT, SWIGLU_LIMIT)
        hdn = (up + 1.0) * gt * jax.nn.sigmoid(SWIGLU_ALPHA * gt)
        y_ref[...] = jnp.dot(hdn.astype(BF16), wd_ref[0], preferred_element_type=F32) + bd_ref[0]

    @pl.when(t >= n_used_ref[0])
    def _():
        y_ref[...] = jnp.zeros_like(y_ref)


def _moe_experts(x1, tile_e, n_used, tok_sorted, wg, bg, wu, bu, wd, bd):
    n_tiles = tile_e.shape[0]
    wspec = pl.BlockSpec((1, D_MODEL, D_FF), lambda t, te, nu: (te[t], 0, 0))
    bspec = pl.BlockSpec((1, 1, D_FF), lambda t, te, nu: (te[t], 0, 0))
    grid_spec = pltpu.PrefetchScalarGridSpec(
        num_scalar_prefetch=2,
        grid=(n_tiles,),
        in_specs=[pl.BlockSpec((1, 1, TM_MOE), lambda t, te, nu: (t, 0, 0), memory_space=pltpu.SMEM),
                  pl.BlockSpec(memory_space=pl.ANY),
                  wspec, bspec, wspec, bspec, wspec, bspec],
        out_specs=pl.BlockSpec((TM_MOE, D_MODEL), lambda t, te, nu: (t, 0)),
        scratch_shapes=[pltpu.VMEM((TM_MOE, D_MODEL), F32), pltpu.SemaphoreType.DMA],
    )
    return pl.pallas_call(
        _moe_body,
        grid_spec=grid_spec,
        out_shape=jax.ShapeDtypeStruct((n_tiles * TM_MOE, D_MODEL), F32),
        compiler_params=_cparams(("arbitrary",)),
        name="moe_experts",
    )(tile_e, n_used, tok_sorted.reshape(n_tiles, 1, TM_MOE), x1, wg, bg, wu, bu, wd, bd)


def _combine_body(pos_ref, x1_ref, gw_ref, l2g_ref, l2b_ref, y_hbm, o_ref, ybuf, sem):
    TM = TM_COMB

    def issue(r, c):
        for k in range(TOP_K):
            pos = pos_ref[0, 0, k * TM + r]
            pltpu.make_async_copy(y_hbm.at[pl.ds(pos, 1), :], ybuf.at[k, pl.ds(r, 1), :], sem).start()
        return c
    lax.fori_loop(0, TM, issue, 0)
    for k in range(TOP_K):
        pltpu.make_async_copy(y_hbm.at[pl.ds(0, TM), :], ybuf.at[k], sem).wait()
    gw = gw_ref[...]
    ffn = gw[:, 0:1] * ybuf[0]
    for k in range(1, TOP_K):
        ffn = ffn + gw[:, k:k + 1] * ybuf[k]
    o_ref[...] = _layer_norm(DEEPNORM_ALPHA * x1_ref[...] + ffn, l2g_ref[...], l2b_ref[...])


def _combine(pos_tiles, x1, gw, l2g, l2b, y_sorted):
    T = x1.shape[0]
    TM = TM_COMB
    full = lambda a: pl.BlockSpec(a.shape, lambda i: (0,) * a.ndim)
    return pl.pallas_call(
        _combine_body,
        grid=(T // TM,),
        in_specs=[pl.BlockSpec((1, 1, TOP_K * TM), lambda i: (i, 0, 0), memory_space=pltpu.SMEM),
                  pl.BlockSpec((TM, D_MODEL), lambda i: (i, 0)),
                  pl.BlockSpec((TM, LANES), lambda i: (i, 0)),
                  full(l2g), full(l2b),
                  pl.BlockSpec(memory_space=pl.ANY)],
        out_specs=pl.BlockSpec((TM, D_MODEL), lambda i: (i, 0)),
        out_shape=jax.ShapeDtypeStruct((T, D_MODEL), F32),
        scratch_shapes=[pltpu.VMEM((TOP_K, TM, D_MODEL), F32), pltpu.SemaphoreType.DMA],
        compiler_params=_cparams(("arbitrary",)),
        name="combine_ln",
    )(pos_tiles, x1, gw, l2g, l2b, y_sorted)


def _route(top_idx, n_tiles):
    T = top_idx.shape[0]
    n_slots = T * TOP_K
    e = top_idx.reshape(n_slots)
    onehot = (e[:, None] == jnp.arange(N_EXPERTS, dtype=jnp.int32)[None, :]).astype(jnp.int32)
    csum = jnp.cumsum(onehot, axis=0)
    rank = jnp.sum(csum * onehot, axis=1) - 1
    counts = csum[-1]
    tiles_e = (counts + TM_MOE - 1) // TM_MOE
    tile_end = jnp.cumsum(tiles_e)
    tile_start = tile_end - tiles_e
    pos = jnp.sum(onehot * (tile_start * TM_MOE)[None, :], axis=1) + rank
    n_used = tile_end[-1:]
    tile_ids = jnp.arange(n_tiles, dtype=jnp.int32)
    tile_e = jnp.sum((tile_ids[:, None] >= tile_end[None, :]).astype(jnp.int32), axis=1)
    tile_e = jnp.minimum(tile_e, N_EXPERTS - 1)
    tok_sorted = jnp.zeros((n_tiles * TM_MOE,), jnp.int32).at[pos].set(
        jnp.arange(n_slots, dtype=jnp.int32) // TOP_K, unique_indices=True)
    return pos, tok_sorted, tile_e.astype(jnp.int32), n_used.astype(jnp.int32)


def _moe_ln(x1, top_idx, gw, wg, bg, wu, bu, wd, bd, l2g, l2b):
    T = x1.shape[0]
    n_tiles = T * TOP_K // TM_MOE + N_EXPERTS
    pos, tok_sorted, tile_e, n_used = _route(top_idx, n_tiles)
    y_sorted = _moe_experts(x1, tile_e, n_used, tok_sorted, wg, bg, wu, bu, wd, bd)
    pos_tiles = pos.reshape(T // TM_COMB, TM_COMB, TOP_K).transpose(0, 2, 1).reshape(T // TM_COMB, 1, TOP_K * TM_COMB)
    return _combine(pos_tiles, x1, gw, l2g, l2b, y_sorted)


def _mixer_ln_router(x, w_in_bf16, wo_bf16, bias_tabs, lb_f, lb_b, ng_tiled, l1g, l1b, wr_hi, wr_lo, br_pad):
    batch, seq_len, _ = x.shape
    x2d = x.reshape(batch * seq_len, D_MODEL)
    q16, kv16, kvn, hh = _inproj(x2d, w_in_bf16)
    attn = _attention(q16, kv16, kvn, bias_tabs, batch, seq_len)
    o_f, o_b = _hgrn(hh, lb_f, lb_b, batch, seq_len)
    return _outproj(attn, o_f, o_b, hh, x2d, wo_bf16, ng_tiled, l1g, l1b, wr_hi, wr_lo, br_pad)


def kernel(x_prompt, x_sample, w_in, w_out, rel_bias, hgrn_lb_fwd, hgrn_lb_bwd, hgrn_norm_g, ln1_g, ln1_b,
           w_router, b_router, w_e_gate, b_e_gate, w_e_up, b_e_up, w_e_down, b_e_down, ln2_g, ln2_b):
    assert DEPTH == 1
    l = 0
    lb_f = jnp.cumsum(jax.nn.softmax(hgrn_lb_fwd.astype(F32), axis=0), axis=0)[l].reshape(1, D_HGRN)
    lb_b = jnp.cumsum(jax.nn.softmax(hgrn_lb_bwd.astype(F32), axis=0), axis=0)[l].reshape(1, D_HGRN)
    col_scale = jnp.concatenate([jnp.full((D_ATTN,), HEAD_DIM_ATTN ** -0.5, F32), jnp.ones((D_IN - D_ATTN,), F32)])
    w_in_bf16 = (w_in[l] * col_scale).astype(BF16)
    wo_bf16 = w_out[l].astype(BF16)
    bias_tabs = _attn_bias_tables(rel_bias)
    ng_tiled = jnp.tile(hgrn_norm_g[l].astype(F32), N_HEADS_HGRN).reshape(1, D_HGRN)
    row = lambda a: a.astype(F32).reshape(1, -1)
    wr = jnp.pad(w_router[l].astype(F32), ((0, 0), (0, LANES - N_EXPERTS)))
    wr_hi, wr_lo = _split_bf16(wr)
    br_pad = jnp.pad(row(b_router[l]), ((0, 0), (0, LANES - N_EXPERTS)), constant_values=NEG_BIG)

    outs = [_mixer_ln_router(x, w_in_bf16, wo_bf16, bias_tabs, lb_f, lb_b, ng_tiled, row(ln1_g[l]), row(ln1_b[l]),
                             wr_hi, wr_lo, br_pad) for x in (x_prompt, x_sample)]
    x1 = jnp.concatenate([o[0] for o in outs], axis=0)
    top_idx = jnp.concatenate([o[1][:, :TOP_K] for o in outs], axis=0)
    gw = jnp.concatenate([o[2] for o in outs], axis=0)

    y = _moe_ln(x1, top_idx, gw,
                w_e_gate[l].astype(BF16), b_e_gate[l].astype(F32).reshape(N_EXPERTS, 1, D_FF),
                w_e_up[l].astype(BF16), b_e_up[l].astype(F32).reshape(N_EXPERTS, 1, D_FF),
                w_e_down[l].astype(BF16), b_e_down[l].astype(F32).reshape(N_EXPERTS, 1, D_MODEL),
                row(ln2_g[l]), row(ln2_b[l]))
    n_p = x_prompt.shape[0] * x_prompt.shape[1]
    return (y[:n_p].reshape(x_prompt.shape), y[n_p:].reshape(x_sample.shape))
```

```python
import functools
import math

import numpy as np
import jax
import jax.numpy as jnp
from jax import lax
from jax.experimental import pallas as pl
from jax.experimental.pallas import tpu as pltpu

D_MODEL = 1024
DEPTH = 1
N_HEADS_ATTN = 8
HEAD_DIM_ATTN = 64
D_ATTN = N_HEADS_ATTN * HEAD_DIM_ATTN
DILATIONS = (1, 4, 16)
HALF_KEYS = 64
NUM_BUCKETS = 32
MAX_DISTANCE = 1024
N_HEADS_HGRN = 4
HEAD_DIM_HGRN = 128
D_HGRN = N_HEADS_HGRN * HEAD_DIM_HGRN
HGRN_CHUNK = 64
D_MIX = D_ATTN + D_HGRN
D_IN = 3 * D_ATTN + 5 * D_HGRN
N_EXPERTS = 32
TOP_K = 4
D_FF = 1024
SWIGLU_LIMIT = 7.0
SWIGLU_ALPHA = 1.702
DEEPNORM_ALPHA = (2.0 * DEPTH) ** 0.25
LN_EPS = 1e-5
NORM_EPS = 1e-6

LANES = 128
VMEM_LIMIT_BYTES = 56 * 2 ** 20

NEG_BIG = -1e30

RES = DILATIONS[-1]
TQ_ATTN = 2048
L_ATTN = TQ_ATTN // RES
UQ = 2 * HALF_KEYS
UK = 4 * HALF_KEYS
N_UNITS = TQ_ATTN // UQ
TM_PROJ = 256
L_PROJ = TM_PROJ // RES
TS_HGRN = 512
TM_OUT = 256
TM_MOE = 512
TM_COMB = 256
N_SLAB = D_MODEL // LANES
MOE_CHUNKS = 4

BF16 = jnp.bfloat16
F32 = jnp.float32


def _cparams(sem):
    return pltpu.CompilerParams(dimension_semantics=sem, vmem_limit_bytes=VMEM_LIMIT_BYTES)


def _inproj_body(x_ref, w_ref, q16_ref, kv16_ref, kvn_ref, hh_ref, xs):
    xb = x_ref[...].astype(BF16)
    nchunk = D_MODEL // LANES
    for c in range(nchunk):
        xs[c] = x_ref[:, c * LANES:(c + 1) * LANES]
    xp = jnp.concatenate(
        [jnp.concatenate([xs.at[c][pl.ds(r, L_PROJ, stride=RES), :] for r in range(RES)], axis=0)
         for c in range(nchunk)], axis=1).astype(BF16)
    kv0, hh0 = D_ATTN, 3 * D_ATTN
    kvn_ref[...] = jnp.dot(xb, w_ref[:, kv0:hh0], preferred_element_type=F32).astype(BF16)
    for n in range(hh0, D_IN, D_HGRN):
        hh_ref[:, n - hh0:n - hh0 + D_HGRN] = jnp.dot(xb, w_ref[:, n:n + D_HGRN], preferred_element_type=F32)
    qp = jnp.dot(xp, w_ref[:, :kv0], preferred_element_type=F32)
    kvp = jnp.dot(xp, w_ref[:, kv0:hh0], preferred_element_type=F32).astype(BF16)
    for r in range(RES):
        q16_ref[0, r] = qp[r * L_PROJ:(r + 1) * L_PROJ, :]
        kv16_ref[0, r] = kvp[r * L_PROJ:(r + 1) * L_PROJ, :]


def _inproj(x2d, w_in_bf16):
    T = x2d.shape[0]
    n_tiles = T // TQ_ATTN
    steps_per_tile = TQ_ATTN // TM_PROJ
    perm_map = lambda i: (i // steps_per_tile, 0, i % steps_per_tile, 0)
    return pl.pallas_call(
        _inproj_body,
        grid=(T // TM_PROJ,),
        in_specs=[pl.BlockSpec((TM_PROJ, D_MODEL), lambda i: (i, 0)),
                  pl.BlockSpec((D_MODEL, D_IN), lambda i: (0, 0))],
        out_specs=[pl.BlockSpec((1, RES, L_PROJ, D_ATTN), perm_map),
                   pl.BlockSpec((1, RES, L_PROJ, 2 * D_ATTN), perm_map),
                   pl.BlockSpec((TM_PROJ, 2 * D_ATTN), lambda i: (i, 0)),
                   pl.BlockSpec((TM_PROJ, 5 * D_HGRN), lambda i: (i, 0))],
        out_shape=[jax.ShapeDtypeStruct((n_tiles, RES, L_ATTN, D_ATTN), F32),
                   jax.ShapeDtypeStruct((n_tiles, RES, L_ATTN, 2 * D_ATTN), BF16),
                   jax.ShapeDtypeStruct((T, 2 * D_ATTN), BF16),
                   jax.ShapeDtypeStruct((T, 5 * D_HGRN), F32)],
        scratch_shapes=[pltpu.VMEM((D_MODEL // LANES, TM_PROJ, LANES), F32)],
        compiler_params=_cparams(("parallel",)),
        name="inproj",
    )(x2d, w_in_bf16)


def _t5_bucket_np(rel):
    half = NUM_BUCKETS // 2
    ret = np.where(rel > 0, half, 0)
    n = np.abs(rel)
    max_exact = half // 2
    nf = np.maximum(n, 1).astype(np.float32)
    large = max_exact + (np.log(nf / np.float32(max_exact)) / np.float32(math.log(MAX_DISTANCE / max_exact))
                         * np.float32(half - max_exact)).astype(np.int32)
    large = np.minimum(large, half - 1)
    return ret + np.where(n < max_exact, n, large)


def _unit_geometry():
    W = HALF_KEYS
    rows = np.arange(UQ)
    cols = np.arange(UK)
    geo = []
    qo = RES * (rows % (UQ // RES)) + rows // (UQ // RES)
    geo.append(((cols[None, :] - W) - qo[:, None], cols < W, cols >= UK - W))
    nq, nk = UQ // 4, UK // 4
    aq, lq = rows // nq, rows % nq
    ak, lk = cols // nk, cols % nk
    rel4 = 4 * (lk[None, :] - W // 4 - lq[:, None]) + (ak[None, :] - aq[:, None])
    geo.append((rel4, lk < W // 4, lk >= nk - W // 4))
    geo.append(((cols[None, :] - W) - rows[:, None], cols < W, cols >= UK - W))
    return geo


def _attn_bias_tables(rel_bias):
    W = HALF_KEYS
    rb = rel_bias.astype(F32)
    per_dil = []
    for d, (rel, prev_cols, next_cols) in zip(DILATIONS, _unit_geometry()):
        bucket = _t5_bucket_np(rel * d)
        b = jnp.zeros((N_HEADS_ATTN, UQ, UK), F32)
        for k in range(NUM_BUCKETS):
            b = jnp.where((bucket == k)[None], rb[k][:, None, None], b)
        band = np.abs(rel) <= W
        variants = [band, band & ~prev_cols[None, :], band & ~next_cols[None, :]]
        per_dil.append(jnp.stack([jnp.where(v[None], b, NEG_BIG) for v in variants], axis=1))
    t = jnp.stack(per_dil, axis=1)
    t = t.reshape(N_HEADS_ATTN // 2, 2, len(DILATIONS), 3, UQ, UK)
    return jnp.transpose(t, (0, 2, 3, 1, 4, 5)).reshape(N_HEADS_ATTN // 2, len(DILATIONS), 3, 2 * UQ, UK)


def _attn_body(q16_ref, kc_ref, kp_ref, kx_ref, vc_ref, vp_ref, vx_ref,
               knc_ref, knp_ref, knx_ref, vnc_ref, vnp_ref, vnx_ref, tab_ref, o_ref,
               kd1, vd1, s_scr, e_scr, acc_p, m_p, den_p, acc_r, m_r, den_r, nat):
    W = HALF_KEYS
    i = pl.program_id(2)
    first = jnp.where(i == 0, 1, 0)
    last = jnp.where(i == pl.num_programs(2) - 1, 2, 0)
    lane = lax.broadcasted_iota(jnp.int32, (1, LANES), 1)
    first_half = lane < HEAD_DIM_ATTN
    ones_v = jnp.ones((UK, LANES), BF16)

    def scores(u, q, k, table):
        qb = q.astype(BF16)
        zero = jnp.zeros_like(qb)
        lhs = jnp.concatenate([jnp.where(first_half, qb, zero), jnp.where(first_half, zero, qb)], axis=0)
        s_scr[u] = lax.dot_general(lhs, k, (((1,), (1,)), ((), ())), preferred_element_type=F32) + table

    def softmax(u):
        s = s_scr[u]
        m = jnp.max(s, axis=-1, keepdims=True)
        e_scr[u] = jnp.exp(s - m).astype(BF16)
        return jnp.where(first_half, m[:UQ], m[UQ:])

    def weighted(u, v):
        r = jnp.dot(e_scr[u], jnp.concatenate([v, ones_v], axis=1), preferred_element_type=F32)
        acc = jnp.where(first_half, r[:UQ, :LANES], r[UQ:, :LANES])
        den = jnp.where(first_half, r[:UQ, LANES:], r[UQ:, LANES:])
        return acc, den

    def merge(p):
        if p == 0:
            acc_r[...] = acc_p[...]
            m_r[...] = m_p[...]
            den_r[...] = den_p[...]
        else:
            m_old = m_r[...]
            m_new_p = m_p[...]
            m_new = jnp.maximum(m_old, m_new_p)
            a = jnp.exp(m_old - m_new)
            b = jnp.exp(m_new_p - m_new)
            acc_r[...] = acc_r[...] * a + acc_p[...] * b
            den_r[...] = den_r[...] * a + den_p[...] * b
            m_r[...] = m_new

    kd1[0:W, :] = knp_ref[...]
    kd1[W:W + TQ_ATTN, :] = knc_ref[...]
    kd1[W + TQ_ATTN:, :] = knx_ref[...]
    vd1[0:W, :] = vnp_ref[...]
    vd1[W:W + TQ_ATTN, :] = vnc_ref[...]
    vd1[W + TQ_ATTN:, :] = vnx_ref[...]
    nl = UQ // RES

    def a1(u, c):
        lo = pl.multiple_of(u * nl, nl)
        q = jnp.concatenate([q16_ref[0, r, pl.ds(lo, nl), :] for r in range(RES)], axis=0)
        k = kd1[pl.ds(pl.multiple_of(u * UQ, UQ), UK), :]
        var = jnp.where(u == 0, first, 0) + jnp.where(u == N_UNITS - 1, last, 0)
        scores(u, q, k, tab_ref[0, 0, var])
        return c

    def b1(u, c):
        m = softmax(u)
        lo = pl.multiple_of(u * nl, nl)
        for r in range(RES):
            m_p[pl.ds(r * L_ATTN + lo, nl), :] = m[r * nl:(r + 1) * nl]
        return c

    def c1(u, c):
        acc, den = weighted(u, vd1[pl.ds(pl.multiple_of(u * UQ, UQ), UK), :])
        lo = pl.multiple_of(u * nl, nl)
        for r in range(RES):
            acc_p[pl.ds(r * L_ATTN + lo, nl), :] = acc[r * nl:(r + 1) * nl]
            den_p[pl.ds(r * L_ATTN + lo, nl), :] = den[r * nl:(r + 1) * nl]
        return c

    lax.fori_loop(0, N_UNITS, a1, 0, unroll=2)
    lax.fori_loop(0, N_UNITS, b1, 0, unroll=2)
    lax.fori_loop(0, N_UNITS, c1, 0, unroll=2)
    merge(0)

    nq, nk = UQ // 4, UK // 4
    nb4 = L_ATTN // nq

    def window4(cur, prev, nxt, r16, nb):
        lo = nb * nq - W // 4
        if lo < 0:
            return [prev[0, r16, W + lo:W, :], cur[0, r16, 0:lo + nk, :]]
        if lo + nk > L_ATTN:
            return [cur[0, r16, lo:L_ATTN, :], nxt[0, r16, 0:lo + nk - L_ATTN, :]]
        return [cur[0, r16, lo:lo + nk, :]]

    def a4(r4, c):
        for nb in range(nb4):
            q = jnp.concatenate([q16_ref[0, r4 + 4 * a, nb * nq:(nb + 1) * nq, :] for a in range(4)], axis=0)
            k = jnp.concatenate(sum([window4(kc_ref, kp_ref, kx_ref, r4 + 4 * a, nb) for a in range(4)], []), axis=0)
            var = first if nb == 0 else (last if nb == nb4 - 1 else 0)
            scores(r4 * nb4 + nb, q, k, tab_ref[0, 1, var])
        return c

    def b4(r4, c):
        for nb in range(nb4):
            m = softmax(r4 * nb4 + nb)
            for a in range(4):
                row0 = pl.multiple_of((r4 + 4 * a) * L_ATTN + nb * nq, nq)
                m_p[pl.ds(row0, nq), :] = m[a * nq:(a + 1) * nq]
        return c

    def c4(r4, c):
        for nb in range(nb4):
            v = jnp.concatenate(sum([window4(vc_ref, vp_ref, vx_ref, r4 + 4 * a, nb) for a in range(4)], []), axis=0)
            acc, den = weighted(r4 * nb4 + nb, v)
            for a in range(4):
                row0 = pl.multiple_of((r4 + 4 * a) * L_ATTN + nb * nq, nq)
                acc_p[pl.ds(row0, nq), :] = acc[a * nq:(a + 1) * nq]
                den_p[pl.ds(row0, nq), :] = den[a * nq:(a + 1) * nq]
        return c

    lax.fori_loop(0, 4, a4, 0)
    lax.fori_loop(0, 4, b4, 0)
    lax.fori_loop(0, 4, c4, 0)
    merge(1)

    var16 = first + last

    def a16(r, c):
        k = jnp.concatenate([kp_ref[0, r], kc_ref[0, r], kx_ref[0, r]], axis=0)
        scores(r, q16_ref[0, r], k, tab_ref[0, 2, var16])
        return c

    def b16(r, c):
        m_p[pl.ds(pl.multiple_of(r * L_ATTN, L_ATTN), L_ATTN), :] = softmax(r)
        return c

    def c16(r, c):
        v = jnp.concatenate([vp_ref[0, r], vc_ref[0, r], vx_ref[0, r]], axis=0)
        acc, den = weighted(r, v)
        rows = pl.ds(pl.multiple_of(r * L_ATTN, L_ATTN), L_ATTN)
        acc_p[rows, :] = acc
        den_p[rows, :] = den
        return c

    lax.fori_loop(0, RES, a16, 0, unroll=2)
    lax.fori_loop(0, RES, b16, 0, unroll=2)
    lax.fori_loop(0, RES, c16, 0, unroll=2)
    merge(2)

    for r in range(RES):
        rows = slice(r * L_ATTN, (r + 1) * L_ATTN)
        nat[pl.ds(r, L_ATTN, stride=RES), :] = acc_r[rows, :] / den_r[rows, :]
    o_ref[...] = nat[...].astype(o_ref.dtype)


def _attention(q16, kv16, kvn, bias_tabs, batch, seq_len):
    TQ, W = TQ_ATTN, HALF_KEYS
    nT = seq_len // TQ
    npairs = N_HEADS_ATTN // 2
    halo_per_tile = TQ // W
    vcol = D_ATTN // LANES

    def tile16(col0, shift, lblock, lsize):
        def imap(b, j, i):
            return (b * nT + jnp.clip(i + shift, 0, nT - 1), 0, lblock, col0 + j)
        return pl.BlockSpec((1, RES, lsize, LANES), imap)

    def nat_cur(col0):
        return pl.BlockSpec((TQ, LANES), lambda b, j, i: (b * nT + i, col0 + j))

    def nat_halo(col0, after):
        def imap(b, j, i):
            blk = (b * nT + i + after) * halo_per_tile - (1 - after)
            return (jnp.clip(blk, b * nT * halo_per_tile, (b + 1) * nT * halo_per_tile - 1), col0 + j)
        return pl.BlockSpec((W, LANES), imap)

    in_specs = [tile16(0, 0, 0, L_ATTN),
                tile16(0, 0, 0, L_ATTN), tile16(0, -1, L_ATTN // W - 1, W), tile16(0, 1, 0, W),
                tile16(vcol, 0, 0, L_ATTN), tile16(vcol, -1, L_ATTN // W - 1, W), tile16(vcol, 1, 0, W),
                nat_cur(0), nat_halo(0, 0), nat_halo(0, 1),
                nat_cur(vcol), nat_halo(vcol, 0), nat_halo(vcol, 1),
                pl.BlockSpec((1, len(DILATIONS), 3, 2 * UQ, UK), lambda b, j, i: (j, 0, 0, 0, 0))]
    scratch = [pltpu.VMEM((TQ + 2 * W, LANES), BF16), pltpu.VMEM((TQ + 2 * W, LANES), BF16),
               pltpu.VMEM((N_UNITS, 2 * UQ, UK), F32), pltpu.VMEM((N_UNITS, 2 * UQ, UK), BF16)]
    scratch += [pltpu.VMEM((TQ, LANES), F32) for _ in range(7)]
    return pl.pallas_call(
        _attn_body,
        grid=(batch, npairs, nT),
        in_specs=in_specs,
        out_specs=pl.BlockSpec((TQ, LANES), lambda b, j, i: (b * nT + i, j)),
        out_shape=jax.ShapeDtypeStruct((batch * seq_len, D_ATTN), BF16),
        scratch_shapes=scratch,
        compiler_params=_cparams(("parallel", "parallel", "arbitrary")),
        name="dilated_attn",
    )(q16, kv16, kv16, kv16, kv16, kv16, kv16, kvn, kvn, kvn, kvn, kvn, kvn, bias_tabs)


def _hgrn_chunk(q, v, z, lb, st_ref, mask, mask_bf, last_row):
    f = lb + (1.0 - lb) * jax.nn.sigmoid(z)
    lf = jnp.log(f)
    kk = 1.0 - f
    hi = lf.astype(BF16)
    lo = (lf - hi.astype(F32)).astype(BF16)
    b = jnp.dot(mask_bf, hi, preferred_element_type=F32) + jnp.dot(mask_bf, lo, preferred_element_type=F32)
    b_last = b[last_row:last_row + 1, :]
    qd = (q * jnp.exp(b) * (HEAD_DIM_HGRN ** -0.5)).astype(BF16)
    kd = (kk * jnp.exp(-b)).astype(BF16)
    kl = (kk * jnp.exp(b_last - b)).astype(BF16)
    vb = v.astype(BF16)
    a = lax.dot_general(qd, kd, (((1,), (1,)), ((), ())), preferred_element_type=F32)
    a = jnp.where(mask, a, 0.0).astype(BF16)
    st = st_ref[...]
    o = jnp.dot(a, vb, preferred_element_type=F32)
    o = o + lax.dot_general(qd, st.astype(BF16), (((1,), (1,)), ((), ())), preferred_element_type=F32)
    upd = lax.dot_general(vb, kl, (((0,), (0,)), ((), ())), preferred_element_type=F32)
    st_ref[...] = st * jnp.exp(b_last) + upd
    return o


def _hgrn_body(qf_ref, vf_ref, zf_ref, qb_ref, vb_ref, zb_ref, lbf_ref, lbb_ref, of_ref, ob_ref, sf, sb):
    C = HGRN_CHUNK
    nch = TS_HGRN // C

    @pl.when(pl.program_id(2) == 0)
    def _():
        sf[...] = jnp.zeros_like(sf)
        sb[...] = jnp.zeros_like(sb)

    row = lax.broadcasted_iota(jnp.int32, (C, C), 0)
    colj = lax.broadcasted_iota(jnp.int32, (C, C), 1)
    tril = row >= colj
    triu = row <= colj
    tril_bf = jnp.where(tril, 1.0, 0.0).astype(BF16)
    triu_bf = jnp.where(triu, 1.0, 0.0).astype(BF16)
    lbf = lbf_ref[...]
    lbb = lbb_ref[...]
    for c in range(nch):
        rf = slice(c * C, (c + 1) * C)
        of_ref[rf, :] = _hgrn_chunk(qf_ref[rf, :], vf_ref[rf, :], zf_ref[rf, :], lbf, sf, tril, tril_bf, C - 1)
        cb = nch - 1 - c
        rb = slice(cb * C, (cb + 1) * C)
        ob_ref[rb, :] = _hgrn_chunk(qb_ref[rb, :], vb_ref[rb, :], zb_ref[rb, :], lbb, sb, triu, triu_bf, 0)


def _hgrn(hh, lb_fwd, lb_bwd, batch, seq_len):
    TS = TS_HGRN
    nT = seq_len // TS
    qcol, zfcol, zbcol, vcol = 0, 4, 8, 12

    def fspec(col0):
        return pl.BlockSpec((TS, LANES), lambda b, hd, i: (b * nT + i, col0 + hd))

    def bspec(col0):
        return pl.BlockSpec((TS, LANES), lambda b, hd, i: (b * nT + nT - 1 - i, col0 + hd))

    lbspec = pl.BlockSpec((1, LANES), lambda b, hd, i: (0, hd))
    out_sds = jax.ShapeDtypeStruct((batch * seq_len, D_HGRN), F32)
    return pl.pallas_call(
        _hgrn_body,
        grid=(batch, N_HEADS_HGRN, nT),
        in_specs=[fspec(qcol), fspec(vcol), fspec(zfcol), bspec(qcol), bspec(vcol), bspec(zbcol), lbspec, lbspec],
        out_specs=[pl.BlockSpec((TS, LANES), lambda b, hd, i: (b * nT + i, hd)),
                   pl.BlockSpec((TS, LANES), lambda b, hd, i: (b * nT + nT - 1 - i, hd))],
        out_shape=[out_sds, out_sds],
        scratch_shapes=[pltpu.VMEM((HEAD_DIM_HGRN, HEAD_DIM_HGRN), F32) for _ in range(2)],
        compiler_params=_cparams(("parallel", "parallel", "arbitrary")),
        name="hgrn2",
    )(hh, hh, hh, hh, hh, hh, lb_fwd, lb_bwd)


def _layer_norm(y, g, b):
    mu = jnp.mean(y, axis=-1, keepdims=True)
    yc = y - mu
    var = jnp.mean(yc * yc, axis=-1, keepdims=True)
    return yc * lax.rsqrt(var + LN_EPS) * g + b


def _split_bf16(a):
    hi = a.astype(BF16)
    lo = (a - hi.astype(F32)).astype(BF16)
    return hi, lo


def _outproj_body(attn_ref, of_ref, ob_ref, g_ref, x_ref, wo_ref, ng_ref, l1g_ref, l1b_ref,
                  wrh_ref, wrl_ref, br_ref, x1_ref, x1t_ref, idx_ref, gw_ref):
    o = of_ref[...] + ob_ref[...]
    parts = []
    for hd in range(N_HEADS_HGRN):
        oh = o[:, hd * HEAD_DIM_HGRN:(hd + 1) * HEAD_DIM_HGRN]
        ms = jnp.mean(oh * oh, axis=-1, keepdims=True)
        parts.append(oh * lax.rsqrt(ms + NORM_EPS))
    g = g_ref[...]
    hg = jnp.concatenate(parts, axis=-1) * ng_ref[...] * (g * jax.nn.sigmoid(g))
    mix = jnp.dot(attn_ref[...], wo_ref[:D_ATTN, :], preferred_element_type=F32)
    mix = mix + jnp.dot(hg.astype(BF16), wo_ref[D_ATTN:, :], preferred_element_type=F32)
    x1 = _layer_norm(DEEPNORM_ALPHA * x_ref[...] + mix, l1g_ref[...], l1b_ref[...])
    x1_ref[...] = x1
    for j in range(N_SLAB):
        x1t_ref[pl.ds(j, TM_OUT, stride=N_SLAB), :] = x1[:, j * LANES:(j + 1) * LANES]

    xh, xl = _split_bf16(x1)
    logits = (jnp.dot(xh, wrh_ref[...], preferred_element_type=F32)
              + jnp.dot(xh, wrl_ref[...], preferred_element_type=F32)
              + jnp.dot(xl, wrh_ref[...], preferred_element_type=F32)) + br_ref[...]
    lane = lax.broadcasted_iota(jnp.int32, logits.shape, 1)
    vals, idxs = [], []
    l = logits
    for _ in range(TOP_K):
        m = jnp.max(l, axis=-1, keepdims=True)
        ik = jnp.min(jnp.where(l == m, lane, LANES), axis=-1, keepdims=True)
        vals.append(m)
        idxs.append(ik)
        l = jnp.where(lane == ik, -jnp.inf, l)
    es = [jnp.exp(v - vals[0]) for v in vals]
    den = es[0] + es[1] + es[2] + es[3]
    idx_out = jnp.zeros(logits.shape, jnp.int32)
    gw_out = jnp.zeros(logits.shape, F32)
    for k in range(TOP_K):
        idx_out = jnp.where(lane == k, idxs[k], idx_out)
        gw_out = jnp.where(lane == k, es[k] / den, gw_out)
    idx_ref[...] = idx_out
    gw_ref[...] = gw_out


def _outproj(attn, o_f, o_b, hh, x2d, wo_bf16, ng_tiled, l1g, l1b, wr_hi, wr_lo, br_pad):
    T = x2d.shape[0]
    TM = TM_OUT
    gcol = 4
    row = lambda w: pl.BlockSpec((TM, w), lambda i: (i, 0))
    full = lambda a: pl.BlockSpec(a.shape, lambda i: (0,) * a.ndim)
    return pl.pallas_call(
        _outproj_body,
        grid=(T // TM,),
        in_specs=[row(D_ATTN), row(D_HGRN), row(D_HGRN),
                  pl.BlockSpec((TM, D_HGRN), lambda i: (i, gcol)),
                  row(D_MODEL), full(wo_bf16), full(ng_tiled), full(l1g), full(l1b),
                  full(wr_hi), full(wr_lo), full(br_pad)],
        out_specs=[row(D_MODEL), pl.BlockSpec((TM * N_SLAB, LANES), lambda i: (i, 0)),
                   row(LANES), row(LANES)],
        out_shape=[jax.ShapeDtypeStruct((T, D_MODEL), F32),
                   jax.ShapeDtypeStruct((T * N_SLAB, LANES), F32),
                   jax.ShapeDtypeStruct((T, LANES), jnp.int32),
                   jax.ShapeDtypeStruct((T, LANES), F32)],
        compiler_params=_cparams(("parallel",)),
        name="outproj_ln_router",
    )(attn, o_f, o_b, hh, x2d, wo_bf16, ng_tiled, l1g, l1b, wr_hi, wr_lo, br_pad)


def _moe_body(tile_e_ref, n_used_ref, tokc_ref, tokn_ref, x1t_hbm, wg_ref, bg_ref, wu_ref, bu_ref, wd_ref, bd_ref,
              y_ref, xbuf, sem):
    t = pl.program_id(0)
    n_used = n_used_ref[0]
    slot = lax.rem(t, 2)

    def row_copy(tok_ref, r, dst):
        src = pl.ds(pl.multiple_of(tok_ref[0, 0, r] * N_SLAB, N_SLAB), N_SLAB)
        rows = pl.ds(pl.multiple_of(r * N_SLAB, N_SLAB), N_SLAB)
        return pltpu.make_async_copy(x1t_hbm.at[src, :], xbuf.at[dst, rows, :], sem.at[dst])

    def wait_rows(dst):
        pltpu.make_async_copy(x1t_hbm.at[pl.ds(0, TM_MOE * N_SLAB), :], xbuf.at[dst], sem.at[dst]).wait()

    @pl.when(t == 0)
    def _():
        def issue(r, c):
            row_copy(tokc_ref, r, 0).start()
            return c
        lax.fori_loop(0, TM_MOE, issue, 0)

    @pl.when(t < n_used)
    def _():
        wait_rows(slot)
        nxt = 1 - slot
        xcur = xbuf.at[slot]
        xb = jnp.concatenate([xcur[pl.ds(j, TM_MOE, stride=N_SLAB), :] for j in range(N_SLAB)],
                             axis=1).astype(BF16)
        per = TM_MOE // (2 * MOE_CHUNKS)
        nc = D_FF // MOE_CHUNKS
        hs = []
        for c in range(MOE_CHUNKS):
            for r in range(c * per, (c + 1) * per):
                row_copy(tokn_ref, r, nxt).start()
            cols = slice(c * nc, (c + 1) * nc)
            gt = jnp.minimum(jnp.dot(xb, wg_ref[0, :, cols], preferred_element_type=F32) + bg_ref[0, :, cols],
                             SWIGLU_LIMIT)
            up = jnp.clip(jnp.dot(xb, wu_ref[0, :, cols], preferred_element_type=F32) + bu_ref[0, :, cols],
                          -SWIGLU_LIMIT, SWIGLU_LIMIT)
            hs.append(((up + 1.0) * gt * jax.nn.sigmoid(SWIGLU_ALPHA * gt)).astype(BF16))
        hdn = jnp.concatenate(hs, axis=1)
        nco = D_MODEL // MOE_CHUNKS
        for c in range(MOE_CHUNKS):
            for r in range((MOE_CHUNKS + c) * per, (MOE_CHUNKS + c + 1) * per):
                row_copy(tokn_ref, r, nxt).start()
            cols = slice(c * nco, (c + 1) * nco)
            y = jnp.dot(hdn, wd_ref[0, :, cols], preferred_element_type=F32) + bd_ref[0, :, cols]
            for jj in range(nco // LANES):
                y_ref[pl.ds(c * (nco // LANES) + jj, TM_MOE, stride=N_SLAB), :] = y[:, jj * LANES:(jj + 1) * LANES]

    @pl.when(t == n_used)
    def _():
        wait_rows(slot)

    @pl.when(t >= n_used)
    def _():
        y_ref[...] = jnp.zeros_like(y_ref)


def _moe_experts(x1t, tile_e, n_used, tok_sorted, wg, bg, wu, bu, wd, bd):
    n_tiles = tile_e.shape[0]
    wspec = pl.BlockSpec((1, D_MODEL, D_FF), lambda t, te, nu: (te[t], 0, 0))
    bspec = pl.BlockSpec((1, 1, D_FF), lambda t, te, nu: (te[t], 0, 0))
    grid_spec = pltpu.PrefetchScalarGridSpec(
        num_scalar_prefetch=2,
        grid=(n_tiles,),
        in_specs=[pl.BlockSpec((1, 1, TM_MOE), lambda t, te, nu: (t, 0, 0), memory_space=pltpu.SMEM),
                  pl.BlockSpec((1, 1, TM_MOE), lambda t, te, nu: (jnp.minimum(t + 1, n_tiles - 1), 0, 0),
                               memory_space=pltpu.SMEM),
                  pl.BlockSpec(memory_space=pl.ANY),
                  wspec, bspec, wspec, bspec, wspec, bspec],
        out_specs=pl.BlockSpec((TM_MOE * N_SLAB, LANES), lambda t, te, nu: (t, 0)),
        scratch_shapes=[pltpu.VMEM((2, TM_MOE * N_SLAB, LANES), F32), pltpu.SemaphoreType.DMA((2,))],
    )
    tok_tiles = tok_sorted.reshape(n_tiles, 1, TM_MOE)
    return pl.pallas_call(
        _moe_body,
        grid_spec=grid_spec,
        out_shape=jax.ShapeDtypeStruct((n_tiles * TM_MOE * N_SLAB, LANES), F32),
        compiler_params=_cparams(("arbitrary",)),
        name="moe_experts",
    )(tile_e, n_used, tok_tiles, tok_tiles, x1t, wg, bg, wu, bu, wd, bd)


def _combine_body(posc_ref, posn_ref, x1_ref, gw_ref, l2g_ref, l2b_ref, y_hbm, o_ref, ybuf, sem):
    TM = TM_COMB
    i = pl.program_id(0)
    slot = lax.rem(i, 2)
    nxt = 1 - slot

    def row_copy(pos_ref, k, r, dst):
        src = pl.ds(pl.multiple_of(pos_ref[0, 0, k * TM + r] * N_SLAB, N_SLAB), N_SLAB)
        rows = pl.ds(pl.multiple_of(r * N_SLAB, N_SLAB), N_SLAB)
        return pltpu.make_async_copy(y_hbm.at[src, :], ybuf.at[dst, k, rows, :], sem.at[dst])

    def wait_rows(dst):
        for k in range(TOP_K):
            pltpu.make_async_copy(y_hbm.at[pl.ds(0, TM * N_SLAB), :], ybuf.at[dst, k], sem.at[dst]).wait()

    @pl.when(i == 0)
    def _():
        def issue(r, c):
            for k in range(TOP_K):
                row_copy(posc_ref, k, r, 0).start()
            return c
        lax.fori_loop(0, TM, issue, 0)

    wait_rows(slot)
    gw = gw_ref[...]
    per = TM // N_SLAB
    gws = [gw[:, k:k + 1] for k in range(TOP_K)]
    s1 = jnp.zeros((TM, 1), F32)
    for j in range(N_SLAB):
        for r in range(j * per, (j + 1) * per):
            for k in range(TOP_K):
                row_copy(posn_ref, k, r, nxt).start()
        cols = slice(j * LANES, (j + 1) * LANES)
        z = DEEPNORM_ALPHA * x1_ref[:, cols]
        for k in range(TOP_K):
            z = z + gws[k] * ybuf.at[slot, k][pl.ds(j, TM, stride=N_SLAB), :]
        o_ref[:, cols] = z
        s1 = s1 + jnp.sum(z, axis=-1, keepdims=True)
    mu = s1 * (1.0 / D_MODEL)
    s2 = jnp.zeros((TM, 1), F32)
    for j in range(N_SLAB):
        zc = o_ref[:, j * LANES:(j + 1) * LANES] - mu
        s2 = s2 + jnp.sum(zc * zc, axis=-1, keepdims=True)
    rstd = lax.rsqrt(s2 * (1.0 / D_MODEL) + LN_EPS)
    for j in range(N_SLAB):
        cols = slice(j * LANES, (j + 1) * LANES)
        o_ref[:, cols] = (o_ref[:, cols] - mu) * rstd * l2g_ref[:, cols] + l2b_ref[:, cols]

    @pl.when(i == pl.num_programs(0) - 1)
    def _():
        wait_rows(nxt)


def _combine(pos_tiles, x1, gw, l2g, l2b, y_sorted):
    T = x1.shape[0]
    TM = TM_COMB
    n = T // TM
    full = lambda a: pl.BlockSpec(a.shape, lambda i: (0,) * a.ndim)
    return pl.pallas_call(
        _combine_body,
        grid=(n,),
        in_specs=[pl.BlockSpec((1, 1, TOP_K * TM), lambda i: (i, 0, 0), memory_space=pltpu.SMEM),
                  pl.BlockSpec((1, 1, TOP_K * TM), lambda i: (jnp.minimum(i + 1, n - 1), 0, 0),
                               memory_space=pltpu.SMEM),
                  pl.BlockSpec((TM, D_MODEL), lambda i: (i, 0)),
                  pl.BlockSpec((TM, LANES), lambda i: (i, 0)),
                  full(l2g), full(l2b),
                  pl.BlockSpec(memory_space=pl.ANY)],
        out_specs=pl.BlockSpec((TM, D_MODEL), lambda i: (i, 0)),
        out_shape=jax.ShapeDtypeStruct((T, D_MODEL), F32),
        scratch_shapes=[pltpu.VMEM((2, TOP_K, TM * N_SLAB, LANES), F32), pltpu.SemaphoreType.DMA((2,))],
        compiler_params=_cparams(("arbitrary",)),
        name="combine_ln",
    )(pos_tiles, pos_tiles, x1, gw, l2g, l2b, y_sorted)


def _route(top_idx, n_tiles):
    T = top_idx.shape[0]
    n_slots = T * TOP_K
    e = top_idx.reshape(n_slots)
    onehot = (e[:, None] == jnp.arange(N_EXPERTS, dtype=jnp.int32)[None, :]).astype(jnp.int32)
    csum = jnp.cumsum(onehot, axis=0)
    rank = jnp.sum(csum * onehot, axis=1) - 1
    counts = csum[-1]
    tiles_e = (counts + TM_MOE - 1) // TM_MOE
    tile_end = jnp.cumsum(tiles_e)
    tile_start = tile_end - tiles_e
    pos = jnp.sum(onehot * (tile_start * TM_MOE)[None, :], axis=1) + rank
    n_used = tile_end[-1:]
    tile_ids = jnp.arange(n_tiles, dtype=jnp.int32)
    tile_e = jnp.sum((tile_ids[:, None] >= tile_end[None, :]).astype(jnp.int32), axis=1)
    tile_e = jnp.minimum(tile_e, N_EXPERTS - 1)
    tok_sorted = jnp.zeros((n_tiles * TM_MOE,), jnp.int32).at[pos].set(
        jnp.arange(n_slots, dtype=jnp.int32) // TOP_K, unique_indices=True)
    return pos, tok_sorted, tile_e.astype(jnp.int32), n_used.astype(jnp.int32)


def _moe_ln(x1, x1t, top_idx, gw, wg, bg, wu, bu, wd, bd, l2g, l2b):
    T = x1.shape[0]
    n_tiles = T * TOP_K // TM_MOE + N_EXPERTS + 1
    pos, tok_sorted, tile_e, n_used = _route(top_idx, n_tiles)
    y_sorted = _moe_experts(x1t, tile_e, n_used, tok_sorted, wg, bg, wu, bu, wd, bd)
    pos_tiles = pos.reshape(T // TM_COMB, TM_COMB, TOP_K).transpose(0, 2, 1).reshape(T // TM_COMB, 1, TOP_K * TM_COMB)
    return _combine(pos_tiles, x1, gw, l2g, l2b, y_sorted)


def _mixer_ln_router(x, w_in_bf16, wo_bf16, bias_tabs, lb_f, lb_b, ng_tiled, l1g, l1b, wr_hi, wr_lo, br_pad):
    batch, seq_len, _ = x.shape
    x2d = x.reshape(batch * seq_len, D_MODEL)
    q16, kv16, kvn, hh = _inproj(x2d, w_in_bf16)
    attn = _attention(q16, kv16, kvn, bias_tabs, batch, seq_len)
    o_f, o_b = _hgrn(hh, lb_f, lb_b, batch, seq_len)
    return _outproj(attn, o_f, o_b, hh, x2d, wo_bf16, ng_tiled, l1g, l1b, wr_hi, wr_lo, br_pad)


def kernel(x_prompt, x_sample, w_in, w_out, rel_bias, hgrn_lb_fwd, hgrn_lb_bwd, hgrn_norm_g, ln1_g, ln1_b,
           w_router, b_router, w_e_gate, b_e_gate, w_e_up, b_e_up, w_e_down, b_e_down, ln2_g, ln2_b):
    assert DEPTH == 1
    l = 0
    lb_f = jnp.cumsum(jax.nn.softmax(hgrn_lb_fwd.astype(F32), axis=0), axis=0)[l].reshape(1, D_HGRN)
    lb_b = jnp.cumsum(jax.nn.softmax(hgrn_lb_bwd.astype(F32), axis=0), axis=0)[l].reshape(1, D_HGRN)
    col_scale = jnp.concatenate([jnp.full((D_ATTN,), HEAD_DIM_ATTN ** -0.5, F32), jnp.ones((D_IN - D_ATTN,), F32)])
    w_in_bf16 = (w_in[l] * col_scale).astype(BF16)
    wo_bf16 = w_out[l].astype(BF16)
    bias_tabs = _attn_bias_tables(rel_bias)
    ng_tiled = jnp.tile(hgrn_norm_g[l].astype(F32), N_HEADS_HGRN).reshape(1, D_HGRN)
    row = lambda a: a.astype(F32).reshape(1, -1)
    wr = jnp.pad(w_router[l].astype(F32), ((0, 0), (0, LANES - N_EXPERTS)))
    wr_hi, wr_lo = _split_bf16(wr)
    br_pad = jnp.pad(row(b_router[l]), ((0, 0), (0, LANES - N_EXPERTS)), constant_values=NEG_BIG)

    outs = [_mixer_ln_router(x, w_in_bf16, wo_bf16, bias_tabs, lb_f, lb_b, ng_tiled, row(ln1_g[l]), row(ln1_b[l]),
                             wr_hi, wr_lo, br_pad) for x in (x_prompt, x_sample)]
    x1 = jnp.concatenate([o[0] for o in outs], axis=0)
    x1t = jnp.concatenate([o[1] for o in outs], axis=0)
    top_idx = jnp.concatenate([o[2][:, :TOP_K] for o in outs], axis=0)
    gw = jnp.concatenate([o[3] for o in outs], axis=0)

    y = _moe_ln(x1, x1t, top_idx, gw,
                w_e_gate[l].astype(BF16), b_e_gate[l].astype(F32).reshape(N_EXPERTS, 1, D_FF),
                w_e_up[l].astype(BF16), b_e_up[l].astype(F32).reshape(N_EXPERTS, 1, D_FF),
                w_e_down[l].astype(BF16), b_e_down[l].astype(F32).reshape(N_EXPERTS, 1, D_MODEL),
                row(ln2_g[l]), row(ln2_b[l]))
    n_p = x_prompt.shape[0] * x_prompt.shape[1]
    return (y[:n_p].reshape(x_prompt.shape), y[n_p:].reshape(x_sample.shape))
```

```python
import functools
import math

import numpy as np
import jax
import jax.numpy as jnp
from jax import lax
from jax.experimental import pallas as pl
from jax.experimental.pallas import tpu as pltpu

D_MODEL = 1024
DEPTH = 1
N_HEADS_ATTN = 8
HEAD_DIM_ATTN = 64
D_ATTN = N_HEADS_ATTN * HEAD_DIM_ATTN
DILATIONS = (1, 4, 16)
HALF_KEYS = 64
NUM_BUCKETS = 32
MAX_DISTANCE = 1024
N_HEADS_HGRN = 4
HEAD_DIM_HGRN = 128
D_HGRN = N_HEADS_HGRN * HEAD_DIM_HGRN
HGRN_CHUNK = 64
D_MIX = D_ATTN + D_HGRN
D_IN = 3 * D_ATTN + 5 * D_HGRN
N_EXPERTS = 32
TOP_K = 4
D_FF = 1024
SWIGLU_LIMIT = 7.0
SWIGLU_ALPHA = 1.702
DEEPNORM_ALPHA = (2.0 * DEPTH) ** 0.25
LN_EPS = 1e-5
NORM_EPS = 1e-6

LANES = 128
VMEM_LIMIT_BYTES = 56 * 2 ** 20

NEG_BIG = -1e30

RES = DILATIONS[-1]
TQ_ATTN = 2048
L_ATTN = TQ_ATTN // RES
UQ = 2 * HALF_KEYS
UK = 4 * HALF_KEYS
N_UNITS = TQ_ATTN // UQ
TM_PROJ = 256
L_PROJ = TM_PROJ // RES
TS_HGRN = 512
TM_OUT = 256
TM_MOE = 512
TM_COMB = 256
N_SLAB = D_MODEL // LANES
MOE_CHUNKS = 4
MOE_SLOTS = 3

BF16 = jnp.bfloat16
F32 = jnp.float32


def _cparams(sem):
    return pltpu.CompilerParams(dimension_semantics=sem, vmem_limit_bytes=VMEM_LIMIT_BYTES)


def _inproj_body(x_ref, w_ref, q16_ref, kv16_ref, kvn_ref, hh_ref, xs):
    xb = x_ref[...].astype(BF16)
    nchunk = D_MODEL // LANES
    for c in range(nchunk):
        xs[c] = x_ref[:, c * LANES:(c + 1) * LANES]
    xp = jnp.concatenate(
        [jnp.concatenate([xs.at[c][pl.ds(r, L_PROJ, stride=RES), :] for r in range(RES)], axis=0)
         for c in range(nchunk)], axis=1).astype(BF16)
    kv0, hh0 = D_ATTN, 3 * D_ATTN
    kvn_ref[...] = jnp.dot(xb, w_ref[:, kv0:hh0], preferred_element_type=F32).astype(BF16)
    for n in range(hh0, D_IN, D_HGRN):
        hh_ref[:, n - hh0:n - hh0 + D_HGRN] = jnp.dot(xb, w_ref[:, n:n + D_HGRN], preferred_element_type=F32)
    qp = jnp.dot(xp, w_ref[:, :kv0], preferred_element_type=F32)
    kvp = jnp.dot(xp, w_ref[:, kv0:hh0], preferred_element_type=F32).astype(BF16)
    for r in range(RES):
        q16_ref[0, r] = qp[r * L_PROJ:(r + 1) * L_PROJ, :]
        kv16_ref[0, r] = kvp[r * L_PROJ:(r + 1) * L_PROJ, :]


def _inproj(x2d, w_in_bf16):
    T = x2d.shape[0]
    n_tiles = T // TQ_ATTN
    steps_per_tile = TQ_ATTN // TM_PROJ
    perm_map = lambda i: (i // steps_per_tile, 0, i % steps_per_tile, 0)
    return pl.pallas_call(
        _inproj_body,
        grid=(T // TM_PROJ,),
        in_specs=[pl.BlockSpec((TM_PROJ, D_MODEL), lambda i: (i, 0)),
                  pl.BlockSpec((D_MODEL, D_IN), lambda i: (0, 0))],
        out_specs=[pl.BlockSpec((1, RES, L_PROJ, D_ATTN), perm_map),
                   pl.BlockSpec((1, RES, L_PROJ, 2 * D_ATTN), perm_map),
                   pl.BlockSpec((TM_PROJ, 2 * D_ATTN), lambda i: (i, 0)),
                   pl.BlockSpec((TM_PROJ, 5 * D_HGRN), lambda i: (i, 0))],
        out_shape=[jax.ShapeDtypeStruct((n_tiles, RES, L_ATTN, D_ATTN), F32),
                   jax.ShapeDtypeStruct((n_tiles, RES, L_ATTN, 2 * D_ATTN), BF16),
                   jax.ShapeDtypeStruct((T, 2 * D_ATTN), BF16),
                   jax.ShapeDtypeStruct((T, 5 * D_HGRN), F32)],
        scratch_shapes=[pltpu.VMEM((D_MODEL // LANES, TM_PROJ, LANES), F32)],
        compiler_params=_cparams(("parallel",)),
        name="inproj",
    )(x2d, w_in_bf16)


def _t5_bucket_np(rel):
    half = NUM_BUCKETS // 2
    ret = np.where(rel > 0, half, 0)
    n = np.abs(rel)
    max_exact = half // 2
    nf = np.maximum(n, 1).astype(np.float32)
    large = max_exact + (np.log(nf / np.float32(max_exact)) / np.float32(math.log(MAX_DISTANCE / max_exact))
                         * np.float32(half - max_exact)).astype(np.int32)
    large = np.minimum(large, half - 1)
    return ret + np.where(n < max_exact, n, large)


def _unit_geometry():
    W = HALF_KEYS
    rows = np.arange(UQ)
    cols = np.arange(UK)
    geo = []
    qo = RES * (rows % (UQ // RES)) + rows // (UQ // RES)
    geo.append(((cols[None, :] - W) - qo[:, None], cols < W, cols >= UK - W))
    nq, nk = UQ // 4, UK // 4
    aq, lq = rows // nq, rows % nq
    ak, lk = cols // nk, cols % nk
    rel4 = 4 * (lk[None, :] - W // 4 - lq[:, None]) + (ak[None, :] - aq[:, None])
    geo.append((rel4, lk < W // 4, lk >= nk - W // 4))
    geo.append(((cols[None, :] - W) - rows[:, None], cols < W, cols >= UK - W))
    return geo


def _attn_bias_tables(rel_bias):
    W = HALF_KEYS
    rb = rel_bias.astype(F32)
    per_dil = []
    for d, (rel, prev_cols, next_cols) in zip(DILATIONS, _unit_geometry()):
        bucket = _t5_bucket_np(rel * d)
        b = jnp.zeros((N_HEADS_ATTN, UQ, UK), F32)
        for k in range(NUM_BUCKETS):
            b = jnp.where((bucket == k)[None], rb[k][:, None, None], b)
        band = np.abs(rel) <= W
        variants = [band, band & ~prev_cols[None, :], band & ~next_cols[None, :]]
        per_dil.append(jnp.stack([jnp.where(v[None], b, NEG_BIG) for v in variants], axis=1))
    t = jnp.stack(per_dil, axis=1)
    t = t.reshape(N_HEADS_ATTN // 2, 2, len(DILATIONS), 3, UQ, UK)
    return jnp.transpose(t, (0, 2, 3, 1, 4, 5)).reshape(N_HEADS_ATTN // 2, len(DILATIONS), 3, 2 * UQ, UK)


def _attn_body(q16_ref, kc_ref, kp_ref, kx_ref, vc_ref, vp_ref, vx_ref,
               knc_ref, knp_ref, knx_ref, vnc_ref, vnp_ref, vnx_ref, tab_ref, o_ref,
               kd1, vd1, s_scr, e_scr, acc_p, m_p, den_p, acc_r, m_r, den_r, nat):
    W = HALF_KEYS
    i = pl.program_id(2)
    first = jnp.where(i == 0, 1, 0)
    last = jnp.where(i == pl.num_programs(2) - 1, 2, 0)
    lane = lax.broadcasted_iota(jnp.int32, (1, LANES), 1)
    first_half = lane < HEAD_DIM_ATTN
    ones_v = jnp.ones((UK, LANES), BF16)

    def scores(u, q, k, table):
        qb = q.astype(BF16)
        zero = jnp.zeros_like(qb)
        lhs = jnp.concatenate([jnp.where(first_half, qb, zero), jnp.where(first_half, zero, qb)], axis=0)
        s_scr[u] = lax.dot_general(lhs, k, (((1,), (1,)), ((), ())), preferred_element_type=F32) + table

    def softmax(u):
        s = s_scr[u]
        m = jnp.max(s, axis=-1, keepdims=True)
        e_scr[u] = jnp.exp(s - m).astype(BF16)
        return jnp.where(first_half, m[:UQ], m[UQ:])

    def weighted(u, v):
        r = jnp.dot(e_scr[u], jnp.concatenate([v, ones_v], axis=1), preferred_element_type=F32)
        acc = jnp.where(first_half, r[:UQ, :LANES], r[UQ:, :LANES])
        den = jnp.where(first_half, r[:UQ, LANES:], r[UQ:, LANES:])
        return acc, den

    def merge(p):
        if p == 0:
            acc_r[...] = acc_p[...]
            m_r[...] = m_p[...]
            den_r[...] = den_p[...]
        else:
            m_old = m_r[...]
            m_new_p = m_p[...]
            m_new = jnp.maximum(m_old, m_new_p)
            a = jnp.exp(m_old - m_new)
            b = jnp.exp(m_new_p - m_new)
            acc_r[...] = acc_r[...] * a + acc_p[...] * b
            den_r[...] = den_r[...] * a + den_p[...] * b
            m_r[...] = m_new

    kd1[0:W, :] = knp_ref[...]
    kd1[W:W + TQ_ATTN, :] = knc_ref[...]
    kd1[W + TQ_ATTN:, :] = knx_ref[...]
    vd1[0:W, :] = vnp_ref[...]
    vd1[W:W + TQ_ATTN, :] = vnc_ref[...]
    vd1[W + TQ_ATTN:, :] = vnx_ref[...]
    nl = UQ // RES

    def a1(u, c):
        lo = pl.multiple_of(u * nl, nl)
        q = jnp.concatenate([q16_ref[0, r, pl.ds(lo, nl), :] for r in range(RES)], axis=0)
        k = kd1[pl.ds(pl.multiple_of(u * UQ, UQ), UK), :]
        var = jnp.where(u == 0, first, 0) + jnp.where(u == N_UNITS - 1, last, 0)
        scores(u, q, k, tab_ref[0, 0, var])
        return c

    def b1(u, c):
        m = softmax(u)
        lo = pl.multiple_of(u * nl, nl)
        for r in range(RES):
            m_p[pl.ds(r * L_ATTN + lo, nl), :] = m[r * nl:(r + 1) * nl]
        return c

    def c1(u, c):
        acc, den = weighted(u, vd1[pl.ds(pl.multiple_of(u * UQ, UQ), UK), :])
        lo = pl.multiple_of(u * nl, nl)
        for r in range(RES):
            acc_p[pl.ds(r * L_ATTN + lo, nl), :] = acc[r * nl:(r + 1) * nl]
            den_p[pl.ds(r * L_ATTN + lo, nl), :] = den[r * nl:(r + 1) * nl]
        return c

    lax.fori_loop(0, N_UNITS, a1, 0, unroll=2)
    lax.fori_loop(0, N_UNITS, b1, 0, unroll=2)
    lax.fori_loop(0, N_UNITS, c1, 0, unroll=2)
    merge(0)

    nq, nk = UQ // 4, UK // 4
    nb4 = L_ATTN // nq

    def window4(cur, prev, nxt, r16, nb):
        lo = nb * nq - W // 4
        if lo < 0:
            return [prev[0, r16, W + lo:W, :], cur[0, r16, 0:lo + nk, :]]
        if lo + nk > L_ATTN:
            return [cur[0, r16, lo:L_ATTN, :], nxt[0, r16, 0:lo + nk - L_ATTN, :]]
        return [cur[0, r16, lo:lo + nk, :]]

    def a4(r4, c):
        for nb in range(nb4):
            q = jnp.concatenate([q16_ref[0, r4 + 4 * a, nb * nq:(nb + 1) * nq, :] for a in range(4)], axis=0)
            k = jnp.concatenate(sum([window4(kc_ref, kp_ref, kx_ref, r4 + 4 * a, nb) for a in range(4)], []), axis=0)
            var = first if nb == 0 else (last if nb == nb4 - 1 else 0)
            scores(r4 * nb4 + nb, q, k, tab_ref[0, 1, var])
        return c

    def b4(r4, c):
        for nb in range(nb4):
            m = softmax(r4 * nb4 + nb)
            for a in range(4):
                row0 = pl.multiple_of((r4 + 4 * a) * L_ATTN + nb * nq, nq)
                m_p[pl.ds(row0, nq), :] = m[a * nq:(a + 1) * nq]
        return c

    def c4(r4, c):
        for nb in range(nb4):
            v = jnp.concatenate(sum([window4(vc_ref, vp_ref, vx_ref, r4 + 4 * a, nb) for a in range(4)], []), axis=0)
            acc, den = weighted(r4 * nb4 + nb, v)
            for a in range(4):
                row0 = pl.multiple_of((r4 + 4 * a) * L_ATTN + nb * nq, nq)
                acc_p[pl.ds(row0, nq), :] = acc[a * nq:(a + 1) * nq]
                den_p[pl.ds(row0, nq), :] = den[a * nq:(a + 1) * nq]
        return c

    lax.fori_loop(0, 4, a4, 0)
    lax.fori_loop(0, 4, b4, 0)
    lax.fori_loop(0, 4, c4, 0)
    merge(1)

    var16 = first + last

    def a16(r, c):
        k = jnp.concatenate([kp_ref[0, r], kc_ref[0, r], kx_ref[0, r]], axis=0)
        scores(r, q16_ref[0, r], k, tab_ref[0, 2, var16])
        return c

    def b16(r, c):
        m_p[pl.ds(pl.multiple_of(r * L_ATTN, L_ATTN), L_ATTN), :] = softmax(r)
        return c

    def c16(r, c):
        v = jnp.concatenate([vp_ref[0, r], vc_ref[0, r], vx_ref[0, r]], axis=0)
        acc, den = weighted(r, v)
        rows = pl.ds(pl.multiple_of(r * L_ATTN, L_ATTN), L_ATTN)
        acc_p[rows, :] = acc
        den_p[rows, :] = den
        return c

    lax.fori_loop(0, RES, a16, 0, unroll=2)
    lax.fori_loop(0, RES, b16, 0, unroll=2)
    lax.fori_loop(0, RES, c16, 0, unroll=2)
    merge(2)

    for r in range(RES):
        rows = slice(r * L_ATTN, (r + 1) * L_ATTN)
        nat[pl.ds(r, L_ATTN, stride=RES), :] = acc_r[rows, :] / den_r[rows, :]
    o_ref[...] = nat[...].astype(o_ref.dtype)


def _attention(q16, kv16, kvn, bias_tabs, batch, seq_len):
    TQ, W = TQ_ATTN, HALF_KEYS
    nT = seq_len // TQ
    npairs = N_HEADS_ATTN // 2
    halo_per_tile = TQ // W
    vcol = D_ATTN // LANES

    def tile16(col0, shift, lblock, lsize):
        def imap(b, j, i):
            return (b * nT + jnp.clip(i + shift, 0, nT - 1), 0, lblock, col0 + j)
        return pl.BlockSpec((1, RES, lsize, LANES), imap)

    def nat_cur(col0):
        return pl.BlockSpec((TQ, LANES), lambda b, j, i: (b * nT + i, col0 + j))

    def nat_halo(col0, after):
        def imap(b, j, i):
            blk = (b * nT + i + after) * halo_per_tile - (1 - after)
            return (jnp.clip(blk, b * nT * halo_per_tile, (b + 1) * nT * halo_per_tile - 1), col0 + j)
        return pl.BlockSpec((W, LANES), imap)

    in_specs = [tile16(0, 0, 0, L_ATTN),
                tile16(0, 0, 0, L_ATTN), tile16(0, -1, L_ATTN // W - 1, W), tile16(0, 1, 0, W),
                tile16(vcol, 0, 0, L_ATTN), tile16(vcol, -1, L_ATTN // W - 1, W), tile16(vcol, 1, 0, W),
                nat_cur(0), nat_halo(0, 0), nat_halo(0, 1),
                nat_cur(vcol), nat_halo(vcol, 0), nat_halo(vcol, 1),
                pl.BlockSpec((1, len(DILATIONS), 3, 2 * UQ, UK), lambda b, j, i: (j, 0, 0, 0, 0))]
    scratch = [pltpu.VMEM((TQ + 2 * W, LANES), BF16), pltpu.VMEM((TQ + 2 * W, LANES), BF16),
               pltpu.VMEM((N_UNITS, 2 * UQ, UK), F32), pltpu.VMEM((N_UNITS, 2 * UQ, UK), BF16)]
    scratch += [pltpu.VMEM((TQ, LANES), F32) for _ in range(7)]
    return pl.pallas_call(
        _attn_body,
        grid=(batch, npairs, nT),
        in_specs=in_specs,
        out_specs=pl.BlockSpec((TQ, LANES), lambda b, j, i: (b * nT + i, j)),
        out_shape=jax.ShapeDtypeStruct((batch * seq_len, D_ATTN), BF16),
        scratch_shapes=scratch,
        compiler_params=_cparams(("parallel", "parallel", "arbitrary")),
        name="dilated_attn",
    )(q16, kv16, kv16, kv16, kv16, kv16, kv16, kvn, kvn, kvn, kvn, kvn, kvn, bias_tabs)


def _hgrn_chunk(q, v, z, lb, st_ref, mask, mask_bf, last_row):
    f = lb + (1.0 - lb) * jax.nn.sigmoid(z)
    lf = jnp.log(f)
    kk = 1.0 - f
    hi = lf.astype(BF16)
    lo = (lf - hi.astype(F32)).astype(BF16)
    b = jnp.dot(mask_bf, hi, preferred_element_type=F32) + jnp.dot(mask_bf, lo, preferred_element_type=F32)
    b_last = b[last_row:last_row + 1, :]
    qd = (q * jnp.exp(b) * (HEAD_DIM_HGRN ** -0.5)).astype(BF16)
    kd = (kk * jnp.exp(-b)).astype(BF16)
    kl = (kk * jnp.exp(b_last - b)).astype(BF16)
    vb = v.astype(BF16)
    a = lax.dot_general(qd, kd, (((1,), (1,)), ((), ())), preferred_element_type=F32)
    a = jnp.where(mask, a, 0.0).astype(BF16)
    st = st_ref[...]
    o = jnp.dot(a, vb, preferred_element_type=F32)
    o = o + lax.dot_general(qd, st.astype(BF16), (((1,), (1,)), ((), ())), preferred_element_type=F32)
    upd = lax.dot_general(vb, kl, (((0,), (0,)), ((), ())), preferred_element_type=F32)
    st_ref[...] = st * jnp.exp(b_last) + upd
    return o


def _hgrn_body(qf_ref, vf_ref, zf_ref, qb_ref, vb_ref, zb_ref, lbf_ref, lbb_ref, of_ref, ob_ref, sf, sb):
    C = HGRN_CHUNK
    nch = TS_HGRN // C

    @pl.when(pl.program_id(2) == 0)
    def _():
        sf[...] = jnp.zeros_like(sf)
        sb[...] = jnp.zeros_like(sb)

    row = lax.broadcasted_iota(jnp.int32, (C, C), 0)
    colj = lax.broadcasted_iota(jnp.int32, (C, C), 1)
    tril = row >= colj
    triu = row <= colj
    tril_bf = jnp.where(tril, 1.0, 0.0).astype(BF16)
    triu_bf = jnp.where(triu, 1.0, 0.0).astype(BF16)
    lbf = lbf_ref[...]
    lbb = lbb_ref[...]
    for c in range(nch):
        rf = slice(c * C, (c + 1) * C)
        of_ref[rf, :] = _hgrn_chunk(qf_ref[rf, :], vf_ref[rf, :], zf_ref[rf, :], lbf, sf, tril, tril_bf, C - 1)
        cb = nch - 1 - c
        rb = slice(cb * C, (cb + 1) * C)
        ob_ref[rb, :] = _hgrn_chunk(qb_ref[rb, :], vb_ref[rb, :], zb_ref[rb, :], lbb, sb, triu, triu_bf, 0)


def _hgrn(hh, lb_fwd, lb_bwd, batch, seq_len):
    TS = TS_HGRN
    nT = seq_len // TS
    qcol, zfcol, zbcol, vcol = 0, 4, 8, 12

    def fspec(col0):
        return pl.BlockSpec((TS, LANES), lambda b, hd, i: (b * nT + i, col0 + hd))

    def bspec(col0):
        return pl.BlockSpec((TS, LANES), lambda b, hd, i: (b * nT + nT - 1 - i, col0 + hd))

    lbspec = pl.BlockSpec((1, LANES), lambda b, hd, i: (0, hd))
    out_sds = jax.ShapeDtypeStruct((batch * seq_len, D_HGRN), F32)
    return pl.pallas_call(
        _hgrn_body,
        grid=(batch, N_HEADS_HGRN, nT),
        in_specs=[fspec(qcol), fspec(vcol), fspec(zfcol), bspec(qcol), bspec(vcol), bspec(zbcol), lbspec, lbspec],
        out_specs=[pl.BlockSpec((TS, LANES), lambda b, hd, i: (b * nT + i, hd)),
                   pl.BlockSpec((TS, LANES), lambda b, hd, i: (b * nT + nT - 1 - i, hd))],
        out_shape=[out_sds, out_sds],
        scratch_shapes=[pltpu.VMEM((HEAD_DIM_HGRN, HEAD_DIM_HGRN), F32) for _ in range(2)],
        compiler_params=_cparams(("parallel", "parallel", "arbitrary")),
        name="hgrn2",
    )(hh, hh, hh, hh, hh, hh, lb_fwd, lb_bwd)


def _layer_norm(y, g, b):
    mu = jnp.mean(y, axis=-1, keepdims=True)
    yc = y - mu
    var = jnp.mean(yc * yc, axis=-1, keepdims=True)
    return yc * lax.rsqrt(var + LN_EPS) * g + b


def _split_bf16(a):
    hi = a.astype(BF16)
    lo = (a - hi.astype(F32)).astype(BF16)
    return hi, lo


def _outproj_body(attn_ref, of_ref, ob_ref, g_ref, x_ref, wo_ref, ng_ref, l1g_ref, l1b_ref,
                  wrh_ref, wrl_ref, br_ref, x1_ref, x1t_ref, idx_ref, gw_ref):
    o = of_ref[...] + ob_ref[...]
    parts = []
    for hd in range(N_HEADS_HGRN):
        oh = o[:, hd * HEAD_DIM_HGRN:(hd + 1) * HEAD_DIM_HGRN]
        ms = jnp.mean(oh * oh, axis=-1, keepdims=True)
        parts.append(oh * lax.rsqrt(ms + NORM_EPS))
    g = g_ref[...]
    hg = jnp.concatenate(parts, axis=-1) * ng_ref[...] * (g * jax.nn.sigmoid(g))
    mix = jnp.dot(attn_ref[...], wo_ref[:D_ATTN, :], preferred_element_type=F32)
    mix = mix + jnp.dot(hg.astype(BF16), wo_ref[D_ATTN:, :], preferred_element_type=F32)
    x1 = _layer_norm(DEEPNORM_ALPHA * x_ref[...] + mix, l1g_ref[...], l1b_ref[...])
    x1_ref[...] = x1
    for j in range(N_SLAB):
        x1t_ref[pl.ds(j, TM_OUT, stride=N_SLAB), :] = x1[:, j * LANES:(j + 1) * LANES]

    xh, xl = _split_bf16(x1)
    logits = (jnp.dot(xh, wrh_ref[...], preferred_element_type=F32)
              + jnp.dot(xh, wrl_ref[...], preferred_element_type=F32)
              + jnp.dot(xl, wrh_ref[...], preferred_element_type=F32)) + br_ref[...]
    lane = lax.broadcasted_iota(jnp.int32, logits.shape, 1)
    vals, idxs = [], []
    l = logits
    for _ in range(TOP_K):
        m = jnp.max(l, axis=-1, keepdims=True)
        ik = jnp.min(jnp.where(l == m, lane, LANES), axis=-1, keepdims=True)
        vals.append(m)
        idxs.append(ik)
        l = jnp.where(lane == ik, -jnp.inf, l)
    es = [jnp.exp(v - vals[0]) for v in vals]
    den = es[0] + es[1] + es[2] + es[3]
    idx_out = jnp.zeros(logits.shape, jnp.int32)
    gw_out = jnp.zeros(logits.shape, F32)
    for k in range(TOP_K):
        idx_out = jnp.where(lane == k, idxs[k], idx_out)
        gw_out = jnp.where(lane == k, es[k] / den, gw_out)
    idx_ref[...] = idx_out
    gw_ref[...] = gw_out


def _outproj(attn, o_f, o_b, hh, x2d, wo_bf16, ng_tiled, l1g, l1b, wr_hi, wr_lo, br_pad):
    T = x2d.shape[0]
    TM = TM_OUT
    gcol = 4
    row = lambda w: pl.BlockSpec((TM, w), lambda i: (i, 0))
    full = lambda a: pl.BlockSpec(a.shape, lambda i: (0,) * a.ndim)
    return pl.pallas_call(
        _outproj_body,
        grid=(T // TM,),
        in_specs=[row(D_ATTN), row(D_HGRN), row(D_HGRN),
                  pl.BlockSpec((TM, D_HGRN), lambda i: (i, gcol)),
                  row(D_MODEL), full(wo_bf16), full(ng_tiled), full(l1g), full(l1b),
                  full(wr_hi), full(wr_lo), full(br_pad)],
        out_specs=[row(D_MODEL), pl.BlockSpec((TM * N_SLAB, LANES), lambda i: (i, 0)),
                   row(LANES), row(LANES)],
        out_shape=[jax.ShapeDtypeStruct((T, D_MODEL), F32),
                   jax.ShapeDtypeStruct((T * N_SLAB, LANES), F32),
                   jax.ShapeDtypeStruct((T, LANES), jnp.int32),
                   jax.ShapeDtypeStruct((T, LANES), F32)],
        compiler_params=_cparams(("parallel",)),
        name="outproj_ln_router",
    )(attn, o_f, o_b, hh, x2d, wo_bf16, ng_tiled, l1g, l1b, wr_hi, wr_lo, br_pad)


def _moe_body(tile_e_ref, n_used_ref, tokc_ref, tok1_ref, tokn_ref, x1t_hbm, wg_ref, bg_ref, wu_ref, bu_ref, wd_ref, bd_ref,
              y_ref, xbuf, wgb, wub, wdb, sem):
    t = pl.program_id(0)
    n_used = n_used_ref[0]
    slot = lax.rem(t, MOE_SLOTS)

    def row_copy(tok_ref, r, dst):
        src = pl.ds(pl.multiple_of(tok_ref[0, 0, r] * N_SLAB, N_SLAB), N_SLAB)
        rows = pl.ds(pl.multiple_of(r * N_SLAB, N_SLAB), N_SLAB)
        return pltpu.make_async_copy(x1t_hbm.at[src, :], xbuf.at[dst, rows, :], sem.at[dst])

    def wait_rows(dst):
        pltpu.make_async_copy(x1t_hbm.at[pl.ds(0, TM_MOE * N_SLAB), :], xbuf.at[dst], sem.at[dst]).wait()

    @pl.when(t == 0)
    def _():
        def issue(r, c):
            row_copy(tokc_ref, r, 0).start()
            row_copy(tok1_ref, r, 1).start()
            return c
        lax.fori_loop(0, TM_MOE, issue, 0)

    @pl.when(t < n_used + (MOE_SLOTS - 1))
    def _():
        wait_rows(slot)

    new_expert = jnp.logical_or(t == 0, tile_e_ref[t] != tile_e_ref[jnp.maximum(t - 1, 0)])

    @pl.when(jnp.logical_and(t < n_used, new_expert))
    def _():
        rows_per = D_MODEL // N_SLAB
        for src, dst in ((wg_ref, wgb), (wu_ref, wub), (wd_ref, wdb)):
            for c in range(N_SLAB):
                rows = slice(c * rows_per, (c + 1) * rows_per)
                dst[rows, :] = src[0, rows, :].astype(BF16)

    @pl.when(t < n_used)
    def _():
        nxt = lax.rem(t + (MOE_SLOTS - 1), MOE_SLOTS)
        xcur = xbuf.at[slot]
        xb = jnp.concatenate([xcur[pl.ds(j, TM_MOE, stride=N_SLAB), :] for j in range(N_SLAB)],
                             axis=1).astype(BF16)
        per = TM_MOE // (2 * MOE_CHUNKS)
        nc = D_FF // MOE_CHUNKS
        hs = []
        for c in range(MOE_CHUNKS):
            for r in range(c * per, (c + 1) * per):
                row_copy(tokn_ref, r, nxt).start()
            cols = slice(c * nc, (c + 1) * nc)
            gt = jnp.minimum(jnp.dot(xb, wgb[:, cols], preferred_element_type=F32) + bg_ref[0, :, cols],
                             SWIGLU_LIMIT)
            up = jnp.clip(jnp.dot(xb, wub[:, cols], preferred_element_type=F32) + bu_ref[0, :, cols],
                          -SWIGLU_LIMIT, SWIGLU_LIMIT)
            hs.append(((up + 1.0) * gt * jax.nn.sigmoid(SWIGLU_ALPHA * gt)).astype(BF16))
        hdn = jnp.concatenate(hs, axis=1)
        nco = D_MODEL // MOE_CHUNKS
        for c in range(MOE_CHUNKS):
            for r in range((MOE_CHUNKS + c) * per, (MOE_CHUNKS + c + 1) * per):
                row_copy(tokn_ref, r, nxt).start()
            cols = slice(c * nco, (c + 1) * nco)
            y = jnp.dot(hdn, wdb[:, cols], preferred_element_type=F32) + bd_ref[0, :, cols]
            for jj in range(nco // LANES):
                y_ref[pl.ds(c * (nco // LANES) + jj, TM_MOE, stride=N_SLAB), :] = y[:, jj * LANES:(jj + 1) * LANES]

    @pl.when(t >= n_used)
    def _():
        y_ref[...] = jnp.zeros_like(y_ref)


def _moe_experts(x1t, tile_e, n_used, tok_sorted, wg, bg, wu, bu, wd, bd):
    n_tiles = tile_e.shape[0]
    wspec = pl.BlockSpec((1, D_MODEL, D_FF), lambda t, te, nu: (te[t], 0, 0))
    bspec = pl.BlockSpec((1, 1, D_FF), lambda t, te, nu: (te[t], 0, 0))

    def tokspec(ahead):
        return pl.BlockSpec((1, 1, TM_MOE), lambda t, te, nu: (jnp.minimum(t + ahead, n_tiles - 1), 0, 0),
                            memory_space=pltpu.SMEM)

    grid_spec = pltpu.PrefetchScalarGridSpec(
        num_scalar_prefetch=2,
        grid=(n_tiles,),
        in_specs=[tokspec(0), tokspec(1), tokspec(MOE_SLOTS - 1),
                  pl.BlockSpec(memory_space=pl.ANY),
                  wspec, bspec, wspec, bspec, wspec, bspec],
        out_specs=pl.BlockSpec((TM_MOE * N_SLAB, LANES), lambda t, te, nu: (t, 0)),
        scratch_shapes=[pltpu.VMEM((MOE_SLOTS, TM_MOE * N_SLAB, LANES), F32),
                        pltpu.VMEM((D_MODEL, D_FF), BF16), pltpu.VMEM((D_MODEL, D_FF), BF16),
                        pltpu.VMEM((D_FF, D_MODEL), BF16),
                        pltpu.SemaphoreType.DMA((MOE_SLOTS,))],
    )
    tok_tiles = tok_sorted.reshape(n_tiles, 1, TM_MOE)
    return pl.pallas_call(
        _moe_body,
        grid_spec=grid_spec,
        out_shape=jax.ShapeDtypeStruct((n_tiles * TM_MOE * N_SLAB, LANES), F32),
        compiler_params=_cparams(("arbitrary",)),
        name="moe_experts",
    )(tile_e, n_used, tok_tiles, tok_tiles, tok_tiles, x1t, wg, bg, wu, bu, wd, bd)


def _combine_body(posc_ref, posn_ref, x1_ref, gw_ref, l2g_ref, l2b_ref, y_hbm, o_ref, ybuf, sem):
    TM = TM_COMB
    i = pl.program_id(0)
    slot = lax.rem(i, 2)
    nxt = 1 - slot

    def row_copy(pos_ref, k, r, dst):
        src = pl.ds(pl.multiple_of(pos_ref[0, 0, k * TM + r] * N_SLAB, N_SLAB), N_SLAB)
        rows = pl.ds(pl.multiple_of(r * N_SLAB, N_SLAB), N_SLAB)
        return pltpu.make_async_copy(y_hbm.at[src, :], ybuf.at[dst, k, rows, :], sem.at[dst])

    def wait_rows(dst):
        for k in range(TOP_K):
            pltpu.make_async_copy(y_hbm.at[pl.ds(0, TM * N_SLAB), :], ybuf.at[dst, k], sem.at[dst]).wait()

    @pl.when(i == 0)
    def _():
        def issue(r, c):
            for k in range(TOP_K):
                row_copy(posc_ref, k, r, 0).start()
            return c
        lax.fori_loop(0, TM, issue, 0)

    wait_rows(slot)
    gw = gw_ref[...]
    per = TM // N_SLAB
    gws = [gw[:, k:k + 1] for k in range(TOP_K)]
    s1 = jnp.zeros((TM, 1), F32)
    for j in range(N_SLAB):
        for r in range(j * per, (j + 1) * per):
            for k in range(TOP_K):
                row_copy(posn_ref, k, r, nxt).start()
        cols = slice(j * LANES, (j + 1) * LANES)
        z = DEEPNORM_ALPHA * x1_ref[:, cols]
        for k in range(TOP_K):
            z = z + gws[k] * ybuf.at[slot, k][pl.ds(j, TM, stride=N_SLAB), :]
        o_ref[:, cols] = z
        s1 = s1 + jnp.sum(z, axis=-1, keepdims=True)
    mu = s1 * (1.0 / D_MODEL)
    s2 = jnp.zeros((TM, 1), F32)
    for j in range(N_SLAB):
        zc = o_ref[:, j * LANES:(j + 1) * LANES] - mu
        s2 = s2 + jnp.sum(zc * zc, axis=-1, keepdims=True)
    rstd = lax.rsqrt(s2 * (1.0 / D_MODEL) + LN_EPS)
    for j in range(N_SLAB):
        cols = slice(j * LANES, (j + 1) * LANES)
        o_ref[:, cols] = (o_ref[:, cols] - mu) * rstd * l2g_ref[:, cols] + l2b_ref[:, cols]

    @pl.when(i == pl.num_programs(0) - 1)
    def _():
        wait_rows(nxt)


def _combine(pos_tiles, x1, gw, l2g, l2b, y_sorted):
    T = x1.shape[0]
    TM = TM_COMB
    n = T // TM
    full = lambda a: pl.BlockSpec(a.shape, lambda i: (0,) * a.ndim)
    return pl.pallas_call(
        _combine_body,
        grid=(n,),
        in_specs=[pl.BlockSpec((1, 1, TOP_K * TM), lambda i: (i, 0, 0), memory_space=pltpu.SMEM),
                  pl.BlockSpec((1, 1, TOP_K * TM), lambda i: (jnp.minimum(i + 1, n - 1), 0, 0),
                               memory_space=pltpu.SMEM),
                  pl.BlockSpec((TM, D_MODEL), lambda i: (i, 0)),
                  pl.BlockSpec((TM, LANES), lambda i: (i, 0)),
                  full(l2g), full(l2b),
                  pl.BlockSpec(memory_space=pl.ANY)],
        out_specs=pl.BlockSpec((TM, D_MODEL), lambda i: (i, 0)),
        out_shape=jax.ShapeDtypeStruct((T, D_MODEL), F32),
        scratch_shapes=[pltpu.VMEM((2, TOP_K, TM * N_SLAB, LANES), F32), pltpu.SemaphoreType.DMA((2,))],
        compiler_params=_cparams(("arbitrary",)),
        name="combine_ln",
    )(pos_tiles, pos_tiles, x1, gw, l2g, l2b, y_sorted)


def _route(top_idx, n_tiles):
    T = top_idx.shape[0]
    n_slots = T * TOP_K
    e = top_idx.reshape(n_slots)
    onehot = (e[:, None] == jnp.arange(N_EXPERTS, dtype=jnp.int32)[None, :]).astype(jnp.int32)
    csum = jnp.cumsum(onehot, axis=0)
    rank = jnp.sum(csum * onehot, axis=1) - 1
    counts = csum[-1]
    tiles_e = (counts + TM_MOE - 1) // TM_MOE
    tile_end = jnp.cumsum(tiles_e)
    tile_start = tile_end - tiles_e
    pos = jnp.sum(onehot * (tile_start * TM_MOE)[None, :], axis=1) + rank
    n_used = tile_end[-1:]
    tile_ids = jnp.arange(n_tiles, dtype=jnp.int32)
    tile_e = jnp.sum((tile_ids[:, None] >= tile_end[None, :]).astype(jnp.int32), axis=1)
    tile_e = jnp.minimum(tile_e, N_EXPERTS - 1)
    tok_sorted = jnp.zeros((n_tiles * TM_MOE,), jnp.int32).at[pos].set(
        jnp.arange(n_slots, dtype=jnp.int32) // TOP_K, unique_indices=True)
    return pos, tok_sorted, tile_e.astype(jnp.int32), n_used.astype(jnp.int32)


def _moe_ln(x1, x1t, top_idx, gw, wg, bg, wu, bu, wd, bd, l2g, l2b):
    T = x1.shape[0]
    n_tiles = T * TOP_K // TM_MOE + N_EXPERTS + (MOE_SLOTS - 1)
    pos, tok_sorted, tile_e, n_used = _route(top_idx, n_tiles)
    y_sorted = _moe_experts(x1t, tile_e, n_used, tok_sorted, wg, bg, wu, bu, wd, bd)
    pos_tiles = pos.reshape(T // TM_COMB, TM_COMB, TOP_K).transpose(0, 2, 1).reshape(T // TM_COMB, 1, TOP_K * TM_COMB)
    return _combine(pos_tiles, x1, gw, l2g, l2b, y_sorted)


def _mixer_ln_router(x, w_in_bf16, wo_bf16, bias_tabs, lb_f, lb_b, ng_tiled, l1g, l1b, wr_hi, wr_lo, br_pad):
    batch, seq_len, _ = x.shape
    x2d = x.reshape(batch * seq_len, D_MODEL)
    q16, kv16, kvn, hh = _inproj(x2d, w_in_bf16)
    attn = _attention(q16, kv16, kvn, bias_tabs, batch, seq_len)
    o_f, o_b = _hgrn(hh, lb_f, lb_b, batch, seq_len)
    return _outproj(attn, o_f, o_b, hh, x2d, wo_bf16, ng_tiled, l1g, l1b, wr_hi, wr_lo, br_pad)


def kernel(x_prompt, x_sample, w_in, w_out, rel_bias, hgrn_lb_fwd, hgrn_lb_bwd, hgrn_norm_g, ln1_g, ln1_b,
           w_router, b_router, w_e_gate, b_e_gate, w_e_up, b_e_up, w_e_down, b_e_down, ln2_g, ln2_b):
    assert DEPTH == 1
    l = 0
    lb_f = jnp.cumsum(jax.nn.softmax(hgrn_lb_fwd.astype(F32), axis=0), axis=0)[l].reshape(1, D_HGRN)
    lb_b = jnp.cumsum(jax.nn.softmax(hgrn_lb_bwd.astype(F32), axis=0), axis=0)[l].reshape(1, D_HGRN)
    col_scale = jnp.concatenate([jnp.full((D_ATTN,), HEAD_DIM_ATTN ** -0.5, F32), jnp.ones((D_IN - D_ATTN,), F32)])
    w_in_bf16 = (w_in[l] * col_scale).astype(BF16)
    wo_bf16 = w_out[l].astype(BF16)
    bias_tabs = _attn_bias_tables(rel_bias)
    ng_tiled = jnp.tile(hgrn_norm_g[l].astype(F32), N_HEADS_HGRN).reshape(1, D_HGRN)
    row = lambda a: a.astype(F32).reshape(1, -1)
    wr = jnp.pad(w_router[l].astype(F32), ((0, 0), (0, LANES - N_EXPERTS)))
    wr_hi, wr_lo = _split_bf16(wr)
    br_pad = jnp.pad(row(b_router[l]), ((0, 0), (0, LANES - N_EXPERTS)), constant_values=NEG_BIG)

    outs = [_mixer_ln_router(x, w_in_bf16, wo_bf16, bias_tabs, lb_f, lb_b, ng_tiled, row(ln1_g[l]), row(ln1_b[l]),
                             wr_hi, wr_lo, br_pad) for x in (x_prompt, x_sample)]
    x1 = jnp.concatenate([o[0] for o in outs], axis=0)
    x1t = jnp.concatenate([o[1] for o in outs], axis=0)
    top_idx = jnp.concatenate([o[2][:, :TOP_K] for o in outs], axis=0)
    gw = jnp.concatenate([o[3] for o in outs], axis=0)

    y = _moe_ln(x1, x1t, top_idx, gw,
                w_e_gate[l].astype(F32), b_e_gate[l].astype(F32).reshape(N_EXPERTS, 1, D_FF),
                w_e_up[l].astype(F32), b_e_up[l].astype(F32).reshape(N_EXPERTS, 1, D_FF),
                w_e_down[l].astype(F32), b_e_down[l].astype(F32).reshape(N_EXPERTS, 1, D_MODEL),
                row(ln2_g[l]), row(ln2_b[l]))
    n_p = x_prompt.shape[0] * x_prompt.shape[1]
    return (y[:n_p].reshape(x_prompt.shape), y[n_p:].reshape(x_sample.shape))
```

```python
import functools
import math

import numpy as np
import jax
import jax.numpy as jnp
from jax import lax
from jax.experimental import pallas as pl
from jax.experimental.pallas import tpu as pltpu

D_MODEL = 1024
DEPTH = 1
N_HEADS_ATTN = 8
HEAD_DIM_ATTN = 64
D_ATTN = N_HEADS_ATTN * HEAD_DIM_ATTN
DILATIONS = (1, 4, 16)
HALF_KEYS = 64
NUM_BUCKETS = 32
MAX_DISTANCE = 1024
N_HEADS_HGRN = 4
HEAD_DIM_HGRN = 128
D_HGRN = N_HEADS_HGRN * HEAD_DIM_HGRN
HGRN_CHUNK = 64
D_MIX = D_ATTN + D_HGRN
D_IN = 3 * D_ATTN + 5 * D_HGRN
N_EXPERTS = 32
TOP_K = 4
D_FF = 1024
SWIGLU_LIMIT = 7.0
SWIGLU_ALPHA = 1.702
DEEPNORM_ALPHA = (2.0 * DEPTH) ** 0.25
LN_EPS = 1e-5
NORM_EPS = 1e-6

LANES = 128
VMEM_LIMIT_BYTES = 56 * 2 ** 20

NEG_BIG = -1e30

RES = DILATIONS[-1]
TQ_ATTN = 2048
L_ATTN = TQ_ATTN // RES
UQ = 2 * HALF_KEYS
UK = 4 * HALF_KEYS
N_UNITS = TQ_ATTN // UQ
TM_PROJ = 256
L_PROJ = TM_PROJ // RES
TS_HGRN = 512
TM_OUT = 256
TM_MOE = 512
TM_COMB = 256
N_SLAB = D_MODEL // LANES
MOE_CHUNKS = 4
MOE_SLOTS = 3

BF16 = jnp.bfloat16
F32 = jnp.float32


def _cparams(sem):
    return pltpu.CompilerParams(dimension_semantics=sem, vmem_limit_bytes=VMEM_LIMIT_BYTES)


def _inproj_body(xa_ref, xb_ref, w_ref, q16_ref, kv16_ref, kvn_ref, hh_ref, xs, *, steps_a):
    nchunk = D_MODEL // LANES
    for src, pred in ((xa_ref, pl.program_id(0) < steps_a), (xb_ref, pl.program_id(0) >= steps_a)):
        @pl.when(pred)
        def _(src=src):
            for c in range(nchunk):
                xs[c] = src[:, c * LANES:(c + 1) * LANES]
    xb = jnp.concatenate([xs[c] for c in range(nchunk)], axis=1).astype(BF16)
    xp = jnp.concatenate(
        [jnp.concatenate([xs.at[c][pl.ds(r, L_PROJ, stride=RES), :] for r in range(RES)], axis=0)
         for c in range(nchunk)], axis=1).astype(BF16)
    kv0, hh0 = D_ATTN, 3 * D_ATTN
    kvn_ref[...] = jnp.dot(xb, w_ref[:, kv0:hh0], preferred_element_type=F32).astype(BF16)
    for n in range(hh0, D_IN, D_HGRN):
        hh_ref[:, n - hh0:n - hh0 + D_HGRN] = jnp.dot(xb, w_ref[:, n:n + D_HGRN], preferred_element_type=F32)
    qp = jnp.dot(xp, w_ref[:, :kv0], preferred_element_type=F32)
    kvp = jnp.dot(xp, w_ref[:, kv0:hh0], preferred_element_type=F32).astype(BF16)
    for r in range(RES):
        q16_ref[0, r] = qp[r * L_PROJ:(r + 1) * L_PROJ, :]
        kv16_ref[0, r] = kvp[r * L_PROJ:(r + 1) * L_PROJ, :]


def _two_source_specs(block, steps_a):
    return (pl.BlockSpec(block, lambda i: (jnp.minimum(i, steps_a - 1), 0)),
            pl.BlockSpec(block, lambda i: (jnp.maximum(i - steps_a, 0), 0)))


def _inproj(xa2d, xb2d, w_in_bf16):
    T = xa2d.shape[0] + xb2d.shape[0]
    steps_a = xa2d.shape[0] // TM_PROJ
    n_tiles = T // TQ_ATTN
    steps_per_tile = TQ_ATTN // TM_PROJ
    perm_map = lambda i: (i // steps_per_tile, 0, i % steps_per_tile, 0)
    return pl.pallas_call(
        functools.partial(_inproj_body, steps_a=steps_a),
        grid=(T // TM_PROJ,),
        in_specs=[*_two_source_specs((TM_PROJ, D_MODEL), steps_a),
                  pl.BlockSpec((D_MODEL, D_IN), lambda i: (0, 0))],
        out_specs=[pl.BlockSpec((1, RES, L_PROJ, D_ATTN), perm_map),
                   pl.BlockSpec((1, RES, L_PROJ, 2 * D_ATTN), perm_map),
                   pl.BlockSpec((TM_PROJ, 2 * D_ATTN), lambda i: (i, 0)),
                   pl.BlockSpec((TM_PROJ, 5 * D_HGRN), lambda i: (i, 0))],
        out_shape=[jax.ShapeDtypeStruct((n_tiles, RES, L_ATTN, D_ATTN), F32),
                   jax.ShapeDtypeStruct((n_tiles, RES, L_ATTN, 2 * D_ATTN), BF16),
                   jax.ShapeDtypeStruct((T, 2 * D_ATTN), BF16),
                   jax.ShapeDtypeStruct((T, 5 * D_HGRN), F32)],
        scratch_shapes=[pltpu.VMEM((D_MODEL // LANES, TM_PROJ, LANES), F32)],
        compiler_params=_cparams(("parallel",)),
        name="inproj",
    )(xa2d, xb2d, w_in_bf16)


def _t5_bucket_np(rel):
    half = NUM_BUCKETS // 2
    ret = np.where(rel > 0, half, 0)
    n = np.abs(rel)
    max_exact = half // 2
    nf = np.maximum(n, 1).astype(np.float32)
    large = max_exact + (np.log(nf / np.float32(max_exact)) / np.float32(math.log(MAX_DISTANCE / max_exact))
                         * np.float32(half - max_exact)).astype(np.int32)
    large = np.minimum(large, half - 1)
    return ret + np.where(n < max_exact, n, large)


def _unit_geometry():
    W = HALF_KEYS
    rows = np.arange(UQ)
    cols = np.arange(UK)
    geo = []
    qo = RES * (rows % (UQ // RES)) + rows // (UQ // RES)
    geo.append(((cols[None, :] - W) - qo[:, None], cols < W, cols >= UK - W))
    nq, nk = UQ // 4, UK // 4
    aq, lq = rows // nq, rows % nq
    ak, lk = cols // nk, cols % nk
    rel4 = 4 * (lk[None, :] - W // 4 - lq[:, None]) + (ak[None, :] - aq[:, None])
    geo.append((rel4, lk < W // 4, lk >= nk - W // 4))
    geo.append(((cols[None, :] - W) - rows[:, None], cols < W, cols >= UK - W))
    return geo


def _attn_bias_tables(rel_bias):
    W = HALF_KEYS
    rb = rel_bias.astype(F32)
    per_dil = []
    for d, (rel, prev_cols, next_cols) in zip(DILATIONS, _unit_geometry()):
        bucket = _t5_bucket_np(rel * d)
        b = jnp.zeros((N_HEADS_ATTN, UQ, UK), F32)
        for k in range(NUM_BUCKETS):
            b = jnp.where((bucket == k)[None], rb[k][:, None, None], b)
        band = np.abs(rel) <= W
        variants = [band, band & ~prev_cols[None, :], band & ~next_cols[None, :]]
        per_dil.append(jnp.stack([jnp.where(v[None], b, NEG_BIG) for v in variants], axis=1))
    t = jnp.stack(per_dil, axis=1)
    t = t.reshape(N_HEADS_ATTN // 2, 2, len(DILATIONS), 3, UQ, UK)
    return jnp.transpose(t, (0, 2, 3, 1, 4, 5)).reshape(N_HEADS_ATTN // 2, len(DILATIONS), 3, 2 * UQ, UK)


def _attn_body(q16_ref, kc_ref, kp_ref, kx_ref, vc_ref, vp_ref, vx_ref,
               knc_ref, knp_ref, knx_ref, vnc_ref, vnp_ref, vnx_ref, tab_ref, o_ref,
               kd1, vd1, s_scr, e_scr, acc_p, m_p, den_p, acc_r, m_r, den_r, nat):
    W = HALF_KEYS
    i = pl.program_id(2)
    first = jnp.where(i == 0, 1, 0)
    last = jnp.where(i == pl.num_programs(2) - 1, 2, 0)
    lane = lax.broadcasted_iota(jnp.int32, (1, LANES), 1)
    first_half = lane < HEAD_DIM_ATTN
    ones_v = jnp.ones((UK, LANES), BF16)

    def scores(u, q, k, table):
        qb = q.astype(BF16)
        zero = jnp.zeros_like(qb)
        lhs = jnp.concatenate([jnp.where(first_half, qb, zero), jnp.where(first_half, zero, qb)], axis=0)
        s_scr[u] = lax.dot_general(lhs, k, (((1,), (1,)), ((), ())), preferred_element_type=F32) + table

    def softmax(u):
        s = s_scr[u]
        m = jnp.max(s, axis=-1, keepdims=True)
        e_scr[u] = jnp.exp(s - m).astype(BF16)
        return jnp.where(first_half, m[:UQ], m[UQ:])

    def weighted(u, v):
        r = jnp.dot(e_scr[u], jnp.concatenate([v, ones_v], axis=1), preferred_element_type=F32)
        acc = jnp.where(first_half, r[:UQ, :LANES], r[UQ:, :LANES])
        den = jnp.where(first_half, r[:UQ, LANES:], r[UQ:, LANES:])
        return acc, den

    def merge(p):
        if p == 0:
            acc_r[...] = acc_p[...]
            m_r[...] = m_p[...]
            den_r[...] = den_p[...]
        else:
            m_old = m_r[...]
            m_new_p = m_p[...]
            m_new = jnp.maximum(m_old, m_new_p)
            a = jnp.exp(m_old - m_new)
            b = jnp.exp(m_new_p - m_new)
            acc_r[...] = acc_r[...] * a + acc_p[...] * b
            den_r[...] = den_r[...] * a + den_p[...] * b
            m_r[...] = m_new

    kd1[0:W, :] = knp_ref[...]
    kd1[W:W + TQ_ATTN, :] = knc_ref[...]
    kd1[W + TQ_ATTN:, :] = knx_ref[...]
    vd1[0:W, :] = vnp_ref[...]
    vd1[W:W + TQ_ATTN, :] = vnc_ref[...]
    vd1[W + TQ_ATTN:, :] = vnx_ref[...]
    nl = UQ // RES

    def a1(u, c):
        lo = pl.multiple_of(u * nl, nl)
        q = jnp.concatenate([q16_ref[0, r, pl.ds(lo, nl), :] for r in range(RES)], axis=0)
        k = kd1[pl.ds(pl.multiple_of(u * UQ, UQ), UK), :]
        var = jnp.where(u == 0, first, 0) + jnp.where(u == N_UNITS - 1, last, 0)
        scores(u, q, k, tab_ref[0, 0, var])
        return c

    def b1(u, c):
        m = softmax(u)
        lo = pl.multiple_of(u * nl, nl)
        for r in range(RES):
            m_p[pl.ds(r * L_ATTN + lo, nl), :] = m[r * nl:(r + 1) * nl]
        return c

    def c1(u, c):
        acc, den = weighted(u, vd1[pl.ds(pl.multiple_of(u * UQ, UQ), UK), :])
        lo = pl.multiple_of(u * nl, nl)
        for r in range(RES):
            acc_p[pl.ds(r * L_ATTN + lo, nl), :] = acc[r * nl:(r + 1) * nl]
            den_p[pl.ds(r * L_ATTN + lo, nl), :] = den[r * nl:(r + 1) * nl]
        return c

    lax.fori_loop(0, N_UNITS, a1, 0, unroll=2)
    lax.fori_loop(0, N_UNITS, b1, 0, unroll=2)
    lax.fori_loop(0, N_UNITS, c1, 0, unroll=2)
    merge(0)

    nq, nk = UQ // 4, UK // 4
    nb4 = L_ATTN // nq

    def window4(cur, prev, nxt, r16, nb):
        lo = nb * nq - W // 4
        if lo < 0:
            return [prev[0, r16, W + lo:W, :], cur[0, r16, 0:lo + nk, :]]
        if lo + nk > L_ATTN:
            return [cur[0, r16, lo:L_ATTN, :], nxt[0, r16, 0:lo + nk - L_ATTN, :]]
        return [cur[0, r16, lo:lo + nk, :]]

    def a4(r4, c):
        for nb in range(nb4):
            q = jnp.concatenate([q16_ref[0, r4 + 4 * a, nb * nq:(nb + 1) * nq, :] for a in range(4)], axis=0)
            k = jnp.concatenate(sum([window4(kc_ref, kp_ref, kx_ref, r4 + 4 * a, nb) for a in range(4)], []), axis=0)
            var = first if nb == 0 else (last if nb == nb4 - 1 else 0)
            scores(r4 * nb4 + nb, q, k, tab_ref[0, 1, var])
        return c

    def b4(r4, c):
        for nb in range(nb4):
            m = softmax(r4 * nb4 + nb)
            for a in range(4):
                row0 = pl.multiple_of((r4 + 4 * a) * L_ATTN + nb * nq, nq)
                m_p[pl.ds(row0, nq), :] = m[a * nq:(a + 1) * nq]
        return c

    def c4(r4, c):
        for nb in range(nb4):
            v = jnp.concatenate(sum([window4(vc_ref, vp_ref, vx_ref, r4 + 4 * a, nb) for a in range(4)], []), axis=0)
            acc, den = weighted(r4 * nb4 + nb, v)
            for a in range(4):
                row0 = pl.multiple_of((r4 + 4 * a) * L_ATTN + nb * nq, nq)
                acc_p[pl.ds(row0, nq), :] = acc[a * nq:(a + 1) * nq]
                den_p[pl.ds(row0, nq), :] = den[a * nq:(a + 1) * nq]
        return c

    lax.fori_loop(0, 4, a4, 0)
    lax.fori_loop(0, 4, b4, 0)
    lax.fori_loop(0, 4, c4, 0)
    merge(1)

    var16 = first + last

    def a16(r, c):
        k = jnp.concatenate([kp_ref[0, r], kc_ref[0, r], kx_ref[0, r]], axis=0)
        scores(r, q16_ref[0, r], k, tab_ref[0, 2, var16])
        return c

    def b16(r, c):
        m_p[pl.ds(pl.multiple_of(r * L_ATTN, L_ATTN), L_ATTN), :] = softmax(r)
        return c

    def c16(r, c):
        v = jnp.concatenate([vp_ref[0, r], vc_ref[0, r], vx_ref[0, r]], axis=0)
        acc, den = weighted(r, v)
        rows = pl.ds(pl.multiple_of(r * L_ATTN, L_ATTN), L_ATTN)
        acc_p[rows, :] = acc
        den_p[rows, :] = den
        return c

    lax.fori_loop(0, RES, a16, 0, unroll=2)
    lax.fori_loop(0, RES, b16, 0, unroll=2)
    lax.fori_loop(0, RES, c16, 0, unroll=2)
    merge(2)

    for r in range(RES):
        rows = slice(r * L_ATTN, (r + 1) * L_ATTN)
        nat[pl.ds(r, L_ATTN, stride=RES), :] = acc_r[rows, :] / den_r[rows, :]
    o_ref[...] = nat[...].astype(o_ref.dtype)


def _attention(q16, kv16, kvn, bias_tabs, batch, seq_len, tile0):
    TQ, W = TQ_ATTN, HALF_KEYS
    nT = seq_len // TQ
    npairs = N_HEADS_ATTN // 2
    halo_per_tile = TQ // W
    vcol = D_ATTN // LANES

    def tile16(col0, shift, lblock, lsize):
        def imap(b, j, i):
            return (tile0 + b * nT + jnp.clip(i + shift, 0, nT - 1), 0, lblock, col0 + j)
        return pl.BlockSpec((1, RES, lsize, LANES), imap)

    def nat_cur(col0):
        return pl.BlockSpec((TQ, LANES), lambda b, j, i: (tile0 + b * nT + i, col0 + j))

    def nat_halo(col0, after):
        def imap(b, j, i):
            seq0 = tile0 + b * nT
            blk = (seq0 + i + after) * halo_per_tile - (1 - after)
            return (jnp.clip(blk, seq0 * halo_per_tile, (seq0 + nT) * halo_per_tile - 1), col0 + j)
        return pl.BlockSpec((W, LANES), imap)

    in_specs = [tile16(0, 0, 0, L_ATTN),
                tile16(0, 0, 0, L_ATTN), tile16(0, -1, L_ATTN // W - 1, W), tile16(0, 1, 0, W),
                tile16(vcol, 0, 0, L_ATTN), tile16(vcol, -1, L_ATTN // W - 1, W), tile16(vcol, 1, 0, W),
                nat_cur(0), nat_halo(0, 0), nat_halo(0, 1),
                nat_cur(vcol), nat_halo(vcol, 0), nat_halo(vcol, 1),
                pl.BlockSpec((1, len(DILATIONS), 3, 2 * UQ, UK), lambda b, j, i: (j, 0, 0, 0, 0))]
    scratch = [pltpu.VMEM((TQ + 2 * W, LANES), BF16), pltpu.VMEM((TQ + 2 * W, LANES), BF16),
               pltpu.VMEM((N_UNITS, 2 * UQ, UK), F32), pltpu.VMEM((N_UNITS, 2 * UQ, UK), BF16)]
    scratch += [pltpu.VMEM((TQ, LANES), F32) for _ in range(7)]
    return pl.pallas_call(
        _attn_body,
        grid=(batch, npairs, nT),
        in_specs=in_specs,
        out_specs=pl.BlockSpec((TQ, LANES), lambda b, j, i: (b * nT + i, j)),
        out_shape=jax.ShapeDtypeStruct((batch * seq_len, D_ATTN), BF16),
        scratch_shapes=scratch,
        compiler_params=_cparams(("parallel", "parallel", "arbitrary")),
        name="dilated_attn",
    )(q16, kv16, kv16, kv16, kv16, kv16, kv16, kvn, kvn, kvn, kvn, kvn, kvn, bias_tabs)


def _hgrn_chunk(q, v, z, lb, st_ref, mask, mask_bf, last_row):
    f = lb + (1.0 - lb) * jax.nn.sigmoid(z)
    lf = jnp.log(f)
    kk = 1.0 - f
    hi = lf.astype(BF16)
    lo = (lf - hi.astype(F32)).astype(BF16)
    b = jnp.dot(mask_bf, hi, preferred_element_type=F32) + jnp.dot(mask_bf, lo, preferred_element_type=F32)
    b_last = b[last_row:last_row + 1, :]
    qd = (q * jnp.exp(b) * (HEAD_DIM_HGRN ** -0.5)).astype(BF16)
    kd = (kk * jnp.exp(-b)).astype(BF16)
    kl = (kk * jnp.exp(b_last - b)).astype(BF16)
    vb = v.astype(BF16)
    a = lax.dot_general(qd, kd, (((1,), (1,)), ((), ())), preferred_element_type=F32)
    a = jnp.where(mask, a, 0.0).astype(BF16)
    st = st_ref[...]
    o = jnp.dot(a, vb, preferred_element_type=F32)
    o = o + lax.dot_general(qd, st.astype(BF16), (((1,), (1,)), ((), ())), preferred_element_type=F32)
    upd = lax.dot_general(vb, kl, (((0,), (0,)), ((), ())), preferred_element_type=F32)
    st_ref[...] = st * jnp.exp(b_last) + upd
    return o


def _hgrn_body(qf_ref, vf_ref, zf_ref, qb_ref, vb_ref, zb_ref, lbf_ref, lbb_ref, of_ref, ob_ref, sf, sb):
    C = HGRN_CHUNK
    nch = TS_HGRN // C

    @pl.when(pl.program_id(2) == 0)
    def _():
        sf[...] = jnp.zeros_like(sf)
        sb[...] = jnp.zeros_like(sb)

    row = lax.broadcasted_iota(jnp.int32, (C, C), 0)
    colj = lax.broadcasted_iota(jnp.int32, (C, C), 1)
    tril = row >= colj
    triu = row <= colj
    tril_bf = jnp.where(tril, 1.0, 0.0).astype(BF16)
    triu_bf = jnp.where(triu, 1.0, 0.0).astype(BF16)
    lbf = lbf_ref[...]
    lbb = lbb_ref[...]
    for c in range(nch):
        rf = slice(c * C, (c + 1) * C)
        of_ref[rf, :] = _hgrn_chunk(qf_ref[rf, :], vf_ref[rf, :], zf_ref[rf, :], lbf, sf, tril, tril_bf, C - 1)
        cb = nch - 1 - c
        rb = slice(cb * C, (cb + 1) * C)
        ob_ref[rb, :] = _hgrn_chunk(qb_ref[rb, :], vb_ref[rb, :], zb_ref[rb, :], lbb, sb, triu, triu_bf, 0)


def _hgrn(hh, lb_fwd, lb_bwd, batch, seq_len, tok0):
    TS = TS_HGRN
    nT = seq_len // TS
    blk0 = tok0 // TS
    qcol, zfcol, zbcol, vcol = 0, 4, 8, 12

    def fspec(col0):
        return pl.BlockSpec((TS, LANES), lambda b, hd, i: (blk0 + b * nT + i, col0 + hd))

    def bspec(col0):
        return pl.BlockSpec((TS, LANES), lambda b, hd, i: (blk0 + b * nT + nT - 1 - i, col0 + hd))

    lbspec = pl.BlockSpec((1, LANES), lambda b, hd, i: (0, hd))
    out_sds = jax.ShapeDtypeStruct((batch * seq_len, D_HGRN), F32)
    return pl.pallas_call(
        _hgrn_body,
        grid=(batch, N_HEADS_HGRN, nT),
        in_specs=[fspec(qcol), fspec(vcol), fspec(zfcol), bspec(qcol), bspec(vcol), bspec(zbcol), lbspec, lbspec],
        out_specs=[pl.BlockSpec((TS, LANES), lambda b, hd, i: (b * nT + i, hd)),
                   pl.BlockSpec((TS, LANES), lambda b, hd, i: (b * nT + nT - 1 - i, hd))],
        out_shape=[out_sds, out_sds],
        scratch_shapes=[pltpu.VMEM((HEAD_DIM_HGRN, HEAD_DIM_HGRN), F32) for _ in range(2)],
        compiler_params=_cparams(("parallel", "parallel", "arbitrary")),
        name="hgrn2",
    )(hh, hh, hh, hh, hh, hh, lb_fwd, lb_bwd)


def _layer_norm(y, g, b):
    mu = jnp.mean(y, axis=-1, keepdims=True)
    yc = y - mu
    var = jnp.mean(yc * yc, axis=-1, keepdims=True)
    return yc * lax.rsqrt(var + LN_EPS) * g + b


def _split_bf16(a):
    hi = a.astype(BF16)
    lo = (a - hi.astype(F32)).astype(BF16)
    return hi, lo


def _outproj_body(attn_a, attn_b, of_a, of_b, ob_a, ob_b, x_a, x_b, g_ref, wo_ref, ng_ref, l1g_ref, l1b_ref,
                  wrh_ref, wrl_ref, br_ref, x1_ref, x1t_ref, idx_ref, gw_ref, *, steps_a):
    from_a = pl.program_id(0) < steps_a
    pick = lambda a_ref, b_ref: jnp.where(from_a, a_ref[...], b_ref[...])
    o = pick(of_a, of_b) + pick(ob_a, ob_b)
    attn = pick(attn_a, attn_b)
    x = pick(x_a, x_b)
    parts = []
    for hd in range(N_HEADS_HGRN):
        oh = o[:, hd * HEAD_DIM_HGRN:(hd + 1) * HEAD_DIM_HGRN]
        ms = jnp.mean(oh * oh, axis=-1, keepdims=True)
        parts.append(oh * lax.rsqrt(ms + NORM_EPS))
    g = g_ref[...]
    hg = jnp.concatenate(parts, axis=-1) * ng_ref[...] * (g * jax.nn.sigmoid(g))
    mix = jnp.dot(attn, wo_ref[:D_ATTN, :], preferred_element_type=F32)
    mix = mix + jnp.dot(hg.astype(BF16), wo_ref[D_ATTN:, :], preferred_element_type=F32)
    x1 = _layer_norm(DEEPNORM_ALPHA * x + mix, l1g_ref[...], l1b_ref[...])
    x1_ref[...] = x1
    for j in range(N_SLAB):
        x1t_ref[pl.ds(j, TM_OUT, stride=N_SLAB), :] = x1[:, j * LANES:(j + 1) * LANES]

    xh, xl = _split_bf16(x1)
    logits = (jnp.dot(xh, wrh_ref[...], preferred_element_type=F32)
              + jnp.dot(xh, wrl_ref[...], preferred_element_type=F32)
              + jnp.dot(xl, wrh_ref[...], preferred_element_type=F32)) + br_ref[...]
    lane = lax.broadcasted_iota(jnp.int32, logits.shape, 1)
    vals, idxs = [], []
    l = logits
    for _ in range(TOP_K):
        m = jnp.max(l, axis=-1, keepdims=True)
        ik = jnp.min(jnp.where(l == m, lane, LANES), axis=-1, keepdims=True)
        vals.append(m)
        idxs.append(ik)
        l = jnp.where(lane == ik, -jnp.inf, l)
    es = [jnp.exp(v - vals[0]) for v in vals]
    den = es[0] + es[1] + es[2] + es[3]
    idx_out = jnp.zeros(logits.shape, jnp.int32)
    gw_out = jnp.zeros(logits.shape, F32)
    for k in range(TOP_K):
        idx_out = jnp.where(lane == k, idxs[k], idx_out)
        gw_out = jnp.where(lane == k, es[k] / den, gw_out)
    idx_ref[...] = idx_out
    gw_ref[...] = gw_out


def _outproj(attn_ab, of_ab, ob_ab, x_ab, hh, wo_bf16, ng_tiled, l1g, l1b, wr_hi, wr_lo, br_pad):
    T = hh.shape[0]
    TM = TM_OUT
    steps_a = x_ab[0].shape[0] // TM
    gcol = 4
    row = lambda w: pl.BlockSpec((TM, w), lambda i: (i, 0))
    full = lambda a: pl.BlockSpec(a.shape, lambda i: (0,) * a.ndim)
    two = lambda w: _two_source_specs((TM, w), steps_a)
    return pl.pallas_call(
        functools.partial(_outproj_body, steps_a=steps_a),
        grid=(T // TM,),
        in_specs=[*two(D_ATTN), *two(D_HGRN), *two(D_HGRN), *two(D_MODEL),
                  pl.BlockSpec((TM, D_HGRN), lambda i: (i, gcol)),
                  full(wo_bf16), full(ng_tiled), full(l1g), full(l1b),
                  full(wr_hi), full(wr_lo), full(br_pad)],
        out_specs=[row(D_MODEL), pl.BlockSpec((TM * N_SLAB, LANES), lambda i: (i, 0)),
                   row(LANES), row(LANES)],
        out_shape=[jax.ShapeDtypeStruct((T, D_MODEL), F32),
                   jax.ShapeDtypeStruct((T * N_SLAB, LANES), F32),
                   jax.ShapeDtypeStruct((T, LANES), jnp.int32),
                   jax.ShapeDtypeStruct((T, LANES), F32)],
        compiler_params=_cparams(("parallel",)),
        name="outproj_ln_router",
    )(*attn_ab, *of_ab, *ob_ab, *x_ab, hh, wo_bf16, ng_tiled, l1g, l1b, wr_hi, wr_lo, br_pad)


def _moe_body(tile_e_ref, n_used_ref, tokc_ref, tok1_ref, tokn_ref, x1t_hbm, wg_ref, bg_ref, wu_ref, bu_ref, wd_ref, bd_ref,
              y_ref, xbuf, wgb, wub, wdb, sem):
    t = pl.program_id(0)
    n_used = n_used_ref[0]
    slot = lax.rem(t, MOE_SLOTS)

    def row_copy(tok_ref, r, dst):
        src = pl.ds(pl.multiple_of(tok_ref[0, 0, r] * N_SLAB, N_SLAB), N_SLAB)
        rows = pl.ds(pl.multiple_of(r * N_SLAB, N_SLAB), N_SLAB)
        return pltpu.make_async_copy(x1t_hbm.at[src, :], xbuf.at[dst, rows, :], sem.at[dst])

    def wait_rows(dst):
        pltpu.make_async_copy(x1t_hbm.at[pl.ds(0, TM_MOE * N_SLAB), :], xbuf.at[dst], sem.at[dst]).wait()

    @pl.when(t == 0)
    def _():
        def issue(r, c):
            row_copy(tokc_ref, r, 0).start()
            row_copy(tok1_ref, r, 1).start()
            return c
        lax.fori_loop(0, TM_MOE, issue, 0)

    @pl.when(t < n_used + (MOE_SLOTS - 1))
    def _():
        wait_rows(slot)

    new_expert = jnp.logical_or(t == 0, tile_e_ref[t] != tile_e_ref[jnp.maximum(t - 1, 0)])

    @pl.when(jnp.logical_and(t < n_used, new_expert))
    def _():
        rows_per = D_MODEL // N_SLAB
        for src, dst in ((wg_ref, wgb), (wu_ref, wub), (wd_ref, wdb)):
            for c in range(N_SLAB):
                rows = slice(c * rows_per, (c + 1) * rows_per)
                dst[rows, :] = src[0, rows, :].astype(BF16)

    @pl.when(t < n_used)
    def _():
        nxt = lax.rem(t + (MOE_SLOTS - 1), MOE_SLOTS)
        xcur = xbuf.at[slot]
        xb = jnp.concatenate([xcur[pl.ds(j, TM_MOE, stride=N_SLAB), :] for j in range(N_SLAB)],
                             axis=1).astype(BF16)
        per = TM_MOE // (2 * MOE_CHUNKS)
        nc = D_FF // MOE_CHUNKS
        hs = []
        for c in range(MOE_CHUNKS):
            for r in range(c * per, (c + 1) * per):
                row_copy(tokn_ref, r, nxt).start()
            cols = slice(c * nc, (c + 1) * nc)
            gt = jnp.minimum(jnp.dot(xb, wgb[:, cols], preferred_element_type=F32) + bg_ref[0, :, cols],
                             SWIGLU_LIMIT)
            up = jnp.clip(jnp.dot(xb, wub[:, cols], preferred_element_type=F32) + bu_ref[0, :, cols],
                          -SWIGLU_LIMIT, SWIGLU_LIMIT)
            hs.append(((up + 1.0) * gt * jax.nn.sigmoid(SWIGLU_ALPHA * gt)).astype(BF16))
        hdn = jnp.concatenate(hs, axis=1)
        nco = D_MODEL // MOE_CHUNKS
        for c in range(MOE_CHUNKS):
            for r in range((MOE_CHUNKS + c) * per, (MOE_CHUNKS + c + 1) * per):
                row_copy(tokn_ref, r, nxt).start()
            cols = slice(c * nco, (c + 1) * nco)
            y = jnp.dot(hdn, wdb[:, cols], preferred_element_type=F32) + bd_ref[0, :, cols]
            for jj in range(nco // LANES):
                y_ref[pl.ds(c * (nco // LANES) + jj, TM_MOE, stride=N_SLAB), :] = y[:, jj * LANES:(jj + 1) * LANES]

    @pl.when(t >= n_used)
    def _():
        y_ref[...] = jnp.zeros_like(y_ref)


def _moe_experts(x1t, tile_e, n_used, tok_sorted, wg, bg, wu, bu, wd, bd):
    n_tiles = tile_e.shape[0]
    wspec = pl.BlockSpec((1, D_MODEL, D_FF), lambda t, te, nu: (te[t], 0, 0))
    bspec = pl.BlockSpec((1, 1, D_FF), lambda t, te, nu: (te[t], 0, 0))

    def tokspec(ahead):
        return pl.BlockSpec((1, 1, TM_MOE), lambda t, te, nu: (jnp.minimum(t + ahead, n_tiles - 1), 0, 0),
                            memory_space=pltpu.SMEM)

    grid_spec = pltpu.PrefetchScalarGridSpec(
        num_scalar_prefetch=2,
        grid=(n_tiles,),
        in_specs=[tokspec(0), tokspec(1), tokspec(MOE_SLOTS - 1),
                  pl.BlockSpec(memory_space=pl.ANY),
                  wspec, bspec, wspec, bspec, wspec, bspec],
        out_specs=pl.BlockSpec((TM_MOE * N_SLAB, LANES), lambda t, te, nu: (t, 0)),
        scratch_shapes=[pltpu.VMEM((MOE_SLOTS, TM_MOE * N_SLAB, LANES), F32),
                        pltpu.VMEM((D_MODEL, D_FF), BF16), pltpu.VMEM((D_MODEL, D_FF), BF16),
                        pltpu.VMEM((D_FF, D_MODEL), BF16),
                        pltpu.SemaphoreType.DMA((MOE_SLOTS,))],
    )
    tok_tiles = tok_sorted.reshape(n_tiles, 1, TM_MOE)
    return pl.pallas_call(
        _moe_body,
        grid_spec=grid_spec,
        out_shape=jax.ShapeDtypeStruct((n_tiles * TM_MOE * N_SLAB, LANES), F32),
        compiler_params=_cparams(("arbitrary",)),
        name="moe_experts",
    )(tile_e, n_used, tok_tiles, tok_tiles, tok_tiles, x1t, wg, bg, wu, bu, wd, bd)


def _combine_body(posc_ref, posn_ref, x1_ref, gw_ref, l2g_ref, l2b_ref, y_hbm, oa_ref, ob_ref, ybuf, zbuf, sem,
                  *, steps_a):
    TM = TM_COMB
    i = pl.program_id(0)
    slot = lax.rem(i, 2)
    nxt = 1 - slot

    def row_copy(pos_ref, k, r, dst):
        src = pl.ds(pl.multiple_of(pos_ref[0, 0, k * TM + r] * N_SLAB, N_SLAB), N_SLAB)
        rows = pl.ds(pl.multiple_of(r * N_SLAB, N_SLAB), N_SLAB)
        return pltpu.make_async_copy(y_hbm.at[src, :], ybuf.at[dst, k, rows, :], sem.at[dst])

    def wait_rows(dst):
        for k in range(TOP_K):
            pltpu.make_async_copy(y_hbm.at[pl.ds(0, TM * N_SLAB), :], ybuf.at[dst, k], sem.at[dst]).wait()

    @pl.when(i == 0)
    def _():
        def issue(r, c):
            for k in range(TOP_K):
                row_copy(posc_ref, k, r, 0).start()
            return c
        lax.fori_loop(0, TM, issue, 0)

    wait_rows(slot)
    gw = gw_ref[...]
    per = TM // N_SLAB
    gws = [gw[:, k:k + 1] for k in range(TOP_K)]
    s1 = jnp.zeros((TM, 1), F32)
    for j in range(N_SLAB):
        for r in range(j * per, (j + 1) * per):
            for k in range(TOP_K):
                row_copy(posn_ref, k, r, nxt).start()
        cols = slice(j * LANES, (j + 1) * LANES)
        z = DEEPNORM_ALPHA * x1_ref[:, cols]
        for k in range(TOP_K):
            z = z + gws[k] * ybuf.at[slot, k][pl.ds(j, TM, stride=N_SLAB), :]
        zbuf[:, cols] = z
        s1 = s1 + jnp.sum(z, axis=-1, keepdims=True)
    mu = s1 * (1.0 / D_MODEL)
    s2 = jnp.zeros((TM, 1), F32)
    for j in range(N_SLAB):
        zc = zbuf[:, j * LANES:(j + 1) * LANES] - mu
        s2 = s2 + jnp.sum(zc * zc, axis=-1, keepdims=True)
    rstd = lax.rsqrt(s2 * (1.0 / D_MODEL) + LN_EPS)
    for o_ref, pred in ((oa_ref, i < steps_a), (ob_ref, i >= steps_a)):
        @pl.when(pred)
        def _(o_ref=o_ref):
            for j in range(N_SLAB):
                cols = slice(j * LANES, (j + 1) * LANES)
                o_ref[:, cols] = (zbuf[:, cols] - mu) * rstd * l2g_ref[:, cols] + l2b_ref[:, cols]

    @pl.when(i == pl.num_programs(0) - 1)
    def _():
        wait_rows(nxt)


def _combine(pos_tiles, x1, gw, l2g, l2b, y_sorted, tokens_a):
    T = x1.shape[0]
    TM = TM_COMB
    n = T // TM
    steps_a = tokens_a // TM
    full = lambda a: pl.BlockSpec(a.shape, lambda i: (0,) * a.ndim)
    return pl.pallas_call(
        functools.partial(_combine_body, steps_a=steps_a),
        grid=(n,),
        in_specs=[pl.BlockSpec((1, 1, TOP_K * TM), lambda i: (i, 0, 0), memory_space=pltpu.SMEM),
                  pl.BlockSpec((1, 1, TOP_K * TM), lambda i: (jnp.minimum(i + 1, n - 1), 0, 0),
                               memory_space=pltpu.SMEM),
                  pl.BlockSpec((TM, D_MODEL), lambda i: (i, 0)),
                  pl.BlockSpec((TM, LANES), lambda i: (i, 0)),
                  full(l2g), full(l2b),
                  pl.BlockSpec(memory_space=pl.ANY)],
        out_specs=list(_two_source_specs((TM, D_MODEL), steps_a)),
        out_shape=[jax.ShapeDtypeStruct((tokens_a, D_MODEL), F32),
                   jax.ShapeDtypeStruct((T - tokens_a, D_MODEL), F32)],
        scratch_shapes=[pltpu.VMEM((2, TOP_K, TM * N_SLAB, LANES), F32), pltpu.VMEM((TM, D_MODEL), F32),
                        pltpu.SemaphoreType.DMA((2,))],
        compiler_params=_cparams(("arbitrary",)),
        name="combine_ln",
    )(pos_tiles, pos_tiles, x1, gw, l2g, l2b, y_sorted)


def _route(top_idx, n_tiles):
    T = top_idx.shape[0]
    n_slots = T * TOP_K
    e = top_idx.reshape(n_slots)
    onehot = (e[:, None] == jnp.arange(N_EXPERTS, dtype=jnp.int32)[None, :]).astype(jnp.int32)
    csum = jnp.cumsum(onehot, axis=0)
    rank = jnp.sum(csum * onehot, axis=1) - 1
    counts = csum[-1]
    tiles_e = (counts + TM_MOE - 1) // TM_MOE
    tile_end = jnp.cumsum(tiles_e)
    tile_start = tile_end - tiles_e
    pos = jnp.sum(onehot * (tile_start * TM_MOE)[None, :], axis=1) + rank
    n_used = tile_end[-1:]
    tile_ids = jnp.arange(n_tiles, dtype=jnp.int32)
    tile_e = jnp.sum((tile_ids[:, None] >= tile_end[None, :]).astype(jnp.int32), axis=1)
    tile_e = jnp.minimum(tile_e, N_EXPERTS - 1)
    n_fill = n_tiles * TM_MOE - n_slots
    gap = tiles_e * TM_MOE - counts
    gap_end = jnp.cumsum(gap)
    d = jnp.arange(n_fill, dtype=jnp.int32)
    ge = jnp.sum((d[:, None] >= gap_end[None, :]).astype(jnp.int32), axis=1)
    ge_hot = (ge[:, None] == jnp.arange(N_EXPERTS, dtype=jnp.int32)[None, :]).astype(jnp.int32)
    in_group = jnp.sum(ge_hot * (tile_start * TM_MOE + counts - (gap_end - gap))[None, :], axis=1) + d
    fill_pos = jnp.where(ge < N_EXPERTS, in_group, n_used[0] * TM_MOE + d - gap_end[-1])
    keys = jnp.concatenate([pos, fill_pos])
    vals = jnp.concatenate([jnp.arange(n_slots, dtype=jnp.int32) // TOP_K, jnp.zeros((n_fill,), jnp.int32)])
    _, tok_sorted = lax.sort((keys, vals), num_keys=1)
    return pos, tok_sorted, tile_e.astype(jnp.int32), n_used.astype(jnp.int32)


def _moe_ln(x1, x1t, top_idx, gw, wg, bg, wu, bu, wd, bd, l2g, l2b, tokens_a):
    T = x1.shape[0]
    n_tiles = T * TOP_K // TM_MOE + N_EXPERTS + (MOE_SLOTS - 1)
    pos, tok_sorted, tile_e, n_used = _route(top_idx, n_tiles)
    y_sorted = _moe_experts(x1t, tile_e, n_used, tok_sorted, wg, bg, wu, bu, wd, bd)
    pos_tiles = pos.reshape(T // TM_COMB, TM_COMB, TOP_K).transpose(0, 2, 1).reshape(T // TM_COMB, 1, TOP_K * TM_COMB)
    return _combine(pos_tiles, x1, gw, l2g, l2b, y_sorted, tokens_a)


def _mixer_ln_router(xs, w_in_bf16, wo_bf16, bias_tabs, lb_f, lb_b, ng_tiled, l1g, l1b, wr_hi, wr_lo, br_pad):
    x2d = [x.reshape(x.shape[0] * x.shape[1], D_MODEL) for x in xs]
    q16, kv16, kvn, hh = _inproj(x2d[0], x2d[1], w_in_bf16)
    attn, o_f, o_b = [], [], []
    tok0 = 0
    for x in xs:
        batch, seq_len, _ = x.shape
        attn.append(_attention(q16, kv16, kvn, bias_tabs, batch, seq_len, tok0 // TQ_ATTN))
        f, b = _hgrn(hh, lb_f, lb_b, batch, seq_len, tok0)
        o_f.append(f)
        o_b.append(b)
        tok0 += batch * seq_len
    return _outproj(attn, o_f, o_b, x2d, hh, wo_bf16, ng_tiled, l1g, l1b, wr_hi, wr_lo, br_pad)


def kernel(x_prompt, x_sample, w_in, w_out, rel_bias, hgrn_lb_fwd, hgrn_lb_bwd, hgrn_norm_g, ln1_g, ln1_b,
           w_router, b_router, w_e_gate, b_e_gate, w_e_up, b_e_up, w_e_down, b_e_down, ln2_g, ln2_b):
    assert DEPTH == 1
    l = 0
    lb_f = jnp.cumsum(jax.nn.softmax(hgrn_lb_fwd.astype(F32), axis=0), axis=0)[l].reshape(1, D_HGRN)
    lb_b = jnp.cumsum(jax.nn.softmax(hgrn_lb_bwd.astype(F32), axis=0), axis=0)[l].reshape(1, D_HGRN)
    col_scale = jnp.concatenate([jnp.full((D_ATTN,), HEAD_DIM_ATTN ** -0.5, F32), jnp.ones((D_IN - D_ATTN,), F32)])
    w_in_bf16 = (w_in[l] * col_scale).astype(BF16)
    wo_bf16 = w_out[l].astype(BF16)
    bias_tabs = _attn_bias_tables(rel_bias)
    ng_tiled = jnp.tile(hgrn_norm_g[l].astype(F32), N_HEADS_HGRN).reshape(1, D_HGRN)
    row = lambda a: a.astype(F32).reshape(1, -1)
    wr = jnp.pad(w_router[l].astype(F32), ((0, 0), (0, LANES - N_EXPERTS)))
    wr_hi, wr_lo = _split_bf16(wr)
    br_pad = jnp.pad(row(b_router[l]), ((0, 0), (0, LANES - N_EXPERTS)), constant_values=NEG_BIG)

    x1, x1t, idx, gw = _mixer_ln_router((x_prompt, x_sample), w_in_bf16, wo_bf16, bias_tabs, lb_f, lb_b, ng_tiled,
                                        row(ln1_g[l]), row(ln1_b[l]), wr_hi, wr_lo, br_pad)
    n_p = x_prompt.shape[0] * x_prompt.shape[1]
    y_p, y_s = _moe_ln(x1, x1t, idx[:, :TOP_K], gw,
                       w_e_gate[l].astype(F32), b_e_gate[l].astype(F32).reshape(N_EXPERTS, 1, D_FF),
                       w_e_up[l].astype(F32), b_e_up[l].astype(F32).reshape(N_EXPERTS, 1, D_FF),
                       w_e_down[l].astype(F32), b_e_down[l].astype(F32).reshape(N_EXPERTS, 1, D_MODEL),
                       row(ln2_g[l]), row(ln2_b[l]), n_p)
    return (y_p.reshape(x_prompt.shape), y_s.reshape(x_sample.shape))
```

```python
import functools
import math

import numpy as np
import jax
import jax.numpy as jnp
from jax import lax
from jax.experimental import pallas as pl
from jax.experimental.pallas import tpu as pltpu

D_MODEL = 1024
DEPTH = 1
N_HEADS_ATTN = 8
HEAD_DIM_ATTN = 64
D_ATTN = N_HEADS_ATTN * HEAD_DIM_ATTN
DILATIONS = (1, 4, 16)
HALF_KEYS = 64
NUM_BUCKETS = 32
MAX_DISTANCE = 1024
N_HEADS_HGRN = 4
HEAD_DIM_HGRN = 128
D_HGRN = N_HEADS_HGRN * HEAD_DIM_HGRN
HGRN_CHUNK = 64
D_MIX = D_ATTN + D_HGRN
D_IN = 3 * D_ATTN + 5 * D_HGRN
N_EXPERTS = 32
TOP_K = 4
D_FF = 1024
SWIGLU_LIMIT = 7.0
SWIGLU_ALPHA = 1.702
DEEPNORM_ALPHA = (2.0 * DEPTH) ** 0.25
LN_EPS = 1e-5
NORM_EPS = 1e-6

LANES = 128
VMEM_LIMIT_BYTES = 56 * 2 ** 20

NEG_BIG = -1e30

RES = DILATIONS[-1]
TQ_ATTN = 2048
L_ATTN = TQ_ATTN // RES
UQ = 2 * HALF_KEYS
UK = 4 * HALF_KEYS
N_UNITS = TQ_ATTN // UQ
TM_PROJ = 256
L_PROJ = TM_PROJ // RES
TS_HGRN = 512
HGRN_HEADS_PER_STEP = 4
TM_OUT = 512
TM_MOE = 512
TM_COMB = 256
N_SLAB = D_MODEL // LANES
MOE_CHUNKS = 4
MOE_SLOTS = 3

BF16 = jnp.bfloat16
F32 = jnp.float32


def _cparams(sem):
    return pltpu.CompilerParams(dimension_semantics=sem, vmem_limit_bytes=VMEM_LIMIT_BYTES)


def _inproj_body(xa_ref, xb_ref, w_ref, q16_ref, kv16_ref, kvn_ref, hh_ref, xs, *, steps_a):
    nchunk = D_MODEL // LANES
    for src, pred in ((xa_ref, pl.program_id(0) < steps_a), (xb_ref, pl.program_id(0) >= steps_a)):
        @pl.when(pred)
        def _(src=src):
            for c in range(nchunk):
                xs[c] = src[:, c * LANES:(c + 1) * LANES]
    xb = jnp.concatenate([xs[c] for c in range(nchunk)], axis=1).astype(BF16)
    xp = jnp.concatenate(
        [jnp.concatenate([xs.at[c][pl.ds(r, L_PROJ, stride=RES), :] for r in range(RES)], axis=0)
         for c in range(nchunk)], axis=1).astype(BF16)
    kv0, hh0 = D_ATTN, 3 * D_ATTN
    kvn_ref[...] = jnp.dot(xb, w_ref[:, kv0:hh0], preferred_element_type=F32).astype(BF16)
    for n in range(hh0, D_IN, D_HGRN):
        hh_ref[:, n - hh0:n - hh0 + D_HGRN] = jnp.dot(xb, w_ref[:, n:n + D_HGRN], preferred_element_type=F32)
    qp = jnp.dot(xp, w_ref[:, :kv0], preferred_element_type=F32)
    kvp = jnp.dot(xp, w_ref[:, kv0:hh0], preferred_element_type=F32).astype(BF16)
    for r in range(RES):
        q16_ref[0, r] = qp[r * L_PROJ:(r + 1) * L_PROJ, :]
        kv16_ref[0, r] = kvp[r * L_PROJ:(r + 1) * L_PROJ, :]


def _two_source_specs(block, steps_a):
    return (pl.BlockSpec(block, lambda i: (jnp.minimum(i, steps_a - 1), 0)),
            pl.BlockSpec(block, lambda i: (jnp.maximum(i - steps_a, 0), 0)))


def _inproj(xa2d, xb2d, w_in_bf16):
    T = xa2d.shape[0] + xb2d.shape[0]
    steps_a = xa2d.shape[0] // TM_PROJ
    n_tiles = T // TQ_ATTN
    steps_per_tile = TQ_ATTN // TM_PROJ
    perm_map = lambda i: (i // steps_per_tile, 0, i % steps_per_tile, 0)
    return pl.pallas_call(
        functools.partial(_inproj_body, steps_a=steps_a),
        grid=(T // TM_PROJ,),
        in_specs=[*_two_source_specs((TM_PROJ, D_MODEL), steps_a),
                  pl.BlockSpec((D_MODEL, D_IN), lambda i: (0, 0))],
        out_specs=[pl.BlockSpec((1, RES, L_PROJ, D_ATTN), perm_map),
                   pl.BlockSpec((1, RES, L_PROJ, 2 * D_ATTN), perm_map),
                   pl.BlockSpec((TM_PROJ, 2 * D_ATTN), lambda i: (i, 0)),
                   pl.BlockSpec((TM_PROJ, 5 * D_HGRN), lambda i: (i, 0))],
        out_shape=[jax.ShapeDtypeStruct((n_tiles, RES, L_ATTN, D_ATTN), F32),
                   jax.ShapeDtypeStruct((n_tiles, RES, L_ATTN, 2 * D_ATTN), BF16),
                   jax.ShapeDtypeStruct((T, 2 * D_ATTN), BF16),
                   jax.ShapeDtypeStruct((T, 5 * D_HGRN), F32)],
        scratch_shapes=[pltpu.VMEM((D_MODEL // LANES, TM_PROJ, LANES), F32)],
        compiler_params=_cparams(("parallel",)),
        name="inproj",
    )(xa2d, xb2d, w_in_bf16)


def _t5_bucket_np(rel):
    half = NUM_BUCKETS // 2
    ret = np.where(rel > 0, half, 0)
    n = np.abs(rel)
    max_exact = half // 2
    nf = np.maximum(n, 1).astype(np.float32)
    large = max_exact + (np.log(nf / np.float32(max_exact)) / np.float32(math.log(MAX_DISTANCE / max_exact))
                         * np.float32(half - max_exact)).astype(np.int32)
    large = np.minimum(large, half - 1)
    return ret + np.where(n < max_exact, n, large)


def _unit_geometry():
    W = HALF_KEYS
    rows = np.arange(UQ)
    cols = np.arange(UK)
    geo = []
    qo = RES * (rows % (UQ // RES)) + rows // (UQ // RES)
    geo.append(((cols[None, :] - W) - qo[:, None], cols < W, cols >= UK - W))
    nq, nk = UQ // 4, UK // 4
    aq, lq = rows // nq, rows % nq
    ak, lk = cols // nk, cols % nk
    rel4 = 4 * (lk[None, :] - W // 4 - lq[:, None]) + (ak[None, :] - aq[:, None])
    geo.append((rel4, lk < W // 4, lk >= nk - W // 4))
    geo.append(((cols[None, :] - W) - rows[:, None], cols < W, cols >= UK - W))
    return geo


def _attn_bias_tables(rel_bias):
    W = HALF_KEYS
    rb = rel_bias.astype(F32)
    per_dil = []
    for d, (rel, prev_cols, next_cols) in zip(DILATIONS, _unit_geometry()):
        bucket = _t5_bucket_np(rel * d)
        b = jnp.zeros((N_HEADS_ATTN, UQ, UK), F32)
        for k in range(NUM_BUCKETS):
            b = jnp.where((bucket == k)[None], rb[k][:, None, None], b)
        band = np.abs(rel) <= W
        variants = [band, band & ~prev_cols[None, :], band & ~next_cols[None, :]]
        per_dil.append(jnp.stack([jnp.where(v[None], b, NEG_BIG) for v in variants], axis=1))
    t = jnp.stack(per_dil, axis=1)
    t = t.reshape(N_HEADS_ATTN // 2, 2, len(DILATIONS), 3, UQ, UK)
    return jnp.transpose(t, (0, 2, 3, 1, 4, 5)).reshape(N_HEADS_ATTN // 2, len(DILATIONS), 3, 2 * UQ, UK)


def _attn_body(q16_ref, kc_ref, kp_ref, kx_ref, vc_ref, vp_ref, vx_ref,
               knc_ref, knp_ref, knx_ref, vnc_ref, vnp_ref, vnx_ref, tab_ref, o_ref,
               kd1, vd1, s_scr, e_scr, m_p, acc_r, m_r, den_r, nat):
    W = HALF_KEYS
    i = pl.program_id(2)
    first = jnp.where(i == 0, 1, 0)
    last = jnp.where(i == pl.num_programs(2) - 1, 2, 0)
    lane = lax.broadcasted_iota(jnp.int32, (1, LANES), 1)
    first_half = lane < HEAD_DIM_ATTN
    ones_v = jnp.ones((UK, LANES), BF16)

    def scores(u, q, k, table):
        qb = q.astype(BF16)
        zero = jnp.zeros_like(qb)
        lhs = jnp.concatenate([jnp.where(first_half, qb, zero), jnp.where(first_half, zero, qb)], axis=0)
        s_scr[u] = lax.dot_general(lhs, k, (((1,), (1,)), ((), ())), preferred_element_type=F32) + table

    def softmax(u):
        s = s_scr[u]
        m = jnp.max(s, axis=-1, keepdims=True)
        e_scr[u] = jnp.exp(s - m).astype(BF16)
        return jnp.where(first_half, m[:UQ], m[UQ:])

    def weighted(u, v):
        r = jnp.dot(e_scr[u], jnp.concatenate([v, ones_v], axis=1), preferred_element_type=F32)
        acc = jnp.where(first_half, r[:UQ, :LANES], r[UQ:, :LANES])
        den = jnp.where(first_half, r[:UQ, LANES:], r[UQ:, LANES:])
        return acc, den

    def merged(rows, acc, den):
        m_old = m_r[rows, :]
        m_new_p = m_p[rows, :]
        m_new = jnp.maximum(m_old, m_new_p)
        a = jnp.exp(m_old - m_new)
        b = jnp.exp(m_new_p - m_new)
        return acc_r[rows, :] * a + acc * b, den_r[rows, :] * a + den * b, m_new

    kd1[0:W, :] = knp_ref[...]
    kd1[W:W + TQ_ATTN, :] = knc_ref[...]
    kd1[W + TQ_ATTN:, :] = knx_ref[...]
    vd1[0:W, :] = vnp_ref[...]
    vd1[W:W + TQ_ATTN, :] = vnc_ref[...]
    vd1[W + TQ_ATTN:, :] = vnx_ref[...]
    nl = UQ // RES

    def a1(u, c):
        lo = pl.multiple_of(u * nl, nl)
        q = jnp.concatenate([q16_ref[0, r, pl.ds(lo, nl), :] for r in range(RES)], axis=0)
        k = kd1[pl.ds(pl.multiple_of(u * UQ, UQ), UK), :]
        var = jnp.where(u == 0, first, 0) + jnp.where(u == N_UNITS - 1, last, 0)
        scores(u, q, k, tab_ref[0, 0, var])
        return c

    def b1(u, c):
        m = softmax(u)
        lo = pl.multiple_of(u * nl, nl)
        for r in range(RES):
            m_r[pl.ds(r * L_ATTN + lo, nl), :] = m[r * nl:(r + 1) * nl]
        return c

    def c1(u, c):
        acc, den = weighted(u, vd1[pl.ds(pl.multiple_of(u * UQ, UQ), UK), :])
        lo = pl.multiple_of(u * nl, nl)
        for r in range(RES):
            acc_r[pl.ds(r * L_ATTN + lo, nl), :] = acc[r * nl:(r + 1) * nl]
            den_r[pl.ds(r * L_ATTN + lo, nl), :] = den[r * nl:(r + 1) * nl]
        return c

    lax.fori_loop(0, N_UNITS, a1, 0, unroll=4)
    lax.fori_loop(0, N_UNITS, b1, 0, unroll=4)
    lax.fori_loop(0, N_UNITS, c1, 0, unroll=4)

    nq, nk = UQ // 4, UK // 4
    nb4 = L_ATTN // nq

    def window4(cur, prev, nxt, r16, nb):
        lo = nb * nq - W // 4
        if lo < 0:
            return [prev[0, r16, W + lo:W, :], cur[0, r16, 0:lo + nk, :]]
        if lo + nk > L_ATTN:
            return [cur[0, r16, lo:L_ATTN, :], nxt[0, r16, 0:lo + nk - L_ATTN, :]]
        return [cur[0, r16, lo:lo + nk, :]]

    def a4(r4, c):
        for nb in range(nb4):
            q = jnp.concatenate([q16_ref[0, r4 + 4 * a, nb * nq:(nb + 1) * nq, :] for a in range(4)], axis=0)
            k = jnp.concatenate(sum([window4(kc_ref, kp_ref, kx_ref, r4 + 4 * a, nb) for a in range(4)], []), axis=0)
            var = first if nb == 0 else (last if nb == nb4 - 1 else 0)
            scores(r4 * nb4 + nb, q, k, tab_ref[0, 1, var])
        return c

    def b4(r4, c):
        for nb in range(nb4):
            m = softmax(r4 * nb4 + nb)
            for a in range(4):
                row0 = pl.multiple_of((r4 + 4 * a) * L_ATTN + nb * nq, nq)
                m_p[pl.ds(row0, nq), :] = m[a * nq:(a + 1) * nq]
        return c

    def c4(r4, c):
        for nb in range(nb4):
            v = jnp.concatenate(sum([window4(vc_ref, vp_ref, vx_ref, r4 + 4 * a, nb) for a in range(4)], []), axis=0)
            acc, den = weighted(r4 * nb4 + nb, v)
            for a in range(4):
                rows = pl.ds(pl.multiple_of((r4 + 4 * a) * L_ATTN + nb * nq, nq), nq)
                seg = slice(a * nq, (a + 1) * nq)
                acc_n, den_n, m_n = merged(rows, acc[seg], den[seg])
                acc_r[rows, :] = acc_n
                den_r[rows, :] = den_n
                m_r[rows, :] = m_n
        return c

    lax.fori_loop(0, 4, a4, 0)
    lax.fori_loop(0, 4, b4, 0)
    lax.fori_loop(0, 4, c4, 0)

    var16 = first + last

    def a16(r, c):
        k = jnp.concatenate([kp_ref[0, r], kc_ref[0, r], kx_ref[0, r]], axis=0)
        scores(r, q16_ref[0, r], k, tab_ref[0, 2, var16])
        return c

    def b16(r, c):
        m_p[pl.ds(pl.multiple_of(r * L_ATTN, L_ATTN), L_ATTN), :] = softmax(r)
        return c

    def c16(r, c):
        v = jnp.concatenate([vp_ref[0, r], vc_ref[0, r], vx_ref[0, r]], axis=0)
        acc, den = weighted(r, v)
        acc_n, den_n, _ = merged(pl.ds(pl.multiple_of(r * L_ATTN, L_ATTN), L_ATTN), acc, den)
        nat[pl.ds(r, L_ATTN, stride=RES), :] = acc_n / den_n
        return c

    lax.fori_loop(0, RES, a16, 0, unroll=4)
    lax.fori_loop(0, RES, b16, 0, unroll=4)
    lax.fori_loop(0, RES, c16, 0, unroll=4)
    o_ref[...] = nat[...].astype(o_ref.dtype)


def _attention(q16, kv16, kvn, bias_tabs, batch, seq_len, tile0):
    TQ, W = TQ_ATTN, HALF_KEYS
    nT = seq_len // TQ
    npairs = N_HEADS_ATTN // 2
    halo_per_tile = TQ // W
    vcol = D_ATTN // LANES

    def tile16(col0, shift, lblock, lsize):
        def imap(b, j, i):
            return (tile0 + b * nT + jnp.clip(i + shift, 0, nT - 1), 0, lblock, col0 + j)
        return pl.BlockSpec((1, RES, lsize, LANES), imap)

    def nat_cur(col0):
        return pl.BlockSpec((TQ, LANES), lambda b, j, i: (tile0 + b * nT + i, col0 + j))

    def nat_halo(col0, after):
        def imap(b, j, i):
            seq0 = tile0 + b * nT
            blk = (seq0 + i + after) * halo_per_tile - (1 - after)
            return (jnp.clip(blk, seq0 * halo_per_tile, (seq0 + nT) * halo_per_tile - 1), col0 + j)
        return pl.BlockSpec((W, LANES), imap)

    in_specs = [tile16(0, 0, 0, L_ATTN),
                tile16(0, 0, 0, L_ATTN), tile16(0, -1, L_ATTN // W - 1, W), tile16(0, 1, 0, W),
                tile16(vcol, 0, 0, L_ATTN), tile16(vcol, -1, L_ATTN // W - 1, W), tile16(vcol, 1, 0, W),
                nat_cur(0), nat_halo(0, 0), nat_halo(0, 1),
                nat_cur(vcol), nat_halo(vcol, 0), nat_halo(vcol, 1),
                pl.BlockSpec((1, len(DILATIONS), 3, 2 * UQ, UK), lambda b, j, i: (j, 0, 0, 0, 0))]
    scratch = [pltpu.VMEM((TQ + 2 * W, LANES), BF16), pltpu.VMEM((TQ + 2 * W, LANES), BF16),
               pltpu.VMEM((N_UNITS, 2 * UQ, UK), F32), pltpu.VMEM((N_UNITS, 2 * UQ, UK), BF16)]
    scratch += [pltpu.VMEM((TQ, LANES), F32) for _ in range(5)]
    return pl.pallas_call(
        _attn_body,
        grid=(batch, npairs, nT),
        in_specs=in_specs,
        out_specs=pl.BlockSpec((TQ, LANES), lambda b, j, i: (b * nT + i, j)),
        out_shape=jax.ShapeDtypeStruct((batch * seq_len, D_ATTN), BF16),
        scratch_shapes=scratch,
        compiler_params=_cparams(("parallel", "parallel", "arbitrary")),
        name="dilated_attn",
    )(q16, kv16, kv16, kv16, kv16, kv16, kv16, kvn, kvn, kvn, kvn, kvn, kvn, bias_tabs)


def _hgrn_chunks(chains):
    nt = (((1,), (1,)), ((), ()))
    tn = (((0,), (0,)), ((), ()))
    gates = []
    for q, v, z, lb, st_ref, mask, mask_bf, last_row in chains:
        f = lb + (1.0 - lb) * jax.nn.sigmoid(z)
        lf = jnp.log(f)
        hi = lf.astype(BF16)
        lo = (lf - hi.astype(F32)).astype(BF16)
        gates.append((1.0 - f, hi, lo))
    bs = [jnp.dot(ch[6], hi, preferred_element_type=F32) + jnp.dot(ch[6], lo, preferred_element_type=F32)
          for ch, (_, hi, lo) in zip(chains, gates)]
    ops = []
    for (q, v, z, lb, st_ref, mask, mask_bf, last_row), (kk, _, _), b in zip(chains, gates, bs):
        b_last = b[last_row:last_row + 1, :]
        qd = (q * jnp.exp(b) * (HEAD_DIM_HGRN ** -0.5)).astype(BF16)
        kd = (kk * jnp.exp(-b)).astype(BF16)
        kl = (kk * jnp.exp(b_last - b)).astype(BF16)
        ops.append((qd, kd, kl, v.astype(BF16), jnp.exp(b_last)))
    scores = [lax.dot_general(qd, kd, nt, preferred_element_type=F32) for qd, kd, _, _, _ in ops]
    sts = [ch[4][...] for ch in chains]
    inter = [lax.dot_general(op[0], st.astype(BF16), nt, preferred_element_type=F32) for op, st in zip(ops, sts)]
    upds = [lax.dot_general(vb, kl, tn, preferred_element_type=F32) for _, _, kl, vb, _ in ops]
    outs = []
    for ch, op, a, o_inter, st, upd in zip(chains, ops, scores, inter, sts, upds):
        a = jnp.where(ch[5], a, 0.0).astype(BF16)
        outs.append(jnp.dot(a, op[3], preferred_element_type=F32) + o_inter)
        ch[4][...] = st * op[4] + upd
    return outs


def _hgrn_body(qf_ref, vf_ref, zf_ref, qb_ref, vb_ref, zb_ref, lbf_ref, lbb_ref, of_ref, ob_ref, sf, sb):
    C = HGRN_CHUNK
    nch = TS_HGRN // C

    @pl.when(pl.program_id(2) == 0)
    def _():
        sf[...] = jnp.zeros_like(sf)
        sb[...] = jnp.zeros_like(sb)

    row = lax.broadcasted_iota(jnp.int32, (C, C), 0)
    colj = lax.broadcasted_iota(jnp.int32, (C, C), 1)
    tril = row >= colj
    triu = row <= colj
    tril_bf = jnp.where(tril, 1.0, 0.0).astype(BF16)
    triu_bf = jnp.where(triu, 1.0, 0.0).astype(BF16)
    for c in range(nch):
        rf = slice(c * C, (c + 1) * C)
        cb = nch - 1 - c
        rb = slice(cb * C, (cb + 1) * C)
        chains, dests = [], []
        for h in range(HGRN_HEADS_PER_STEP):
            hl = slice(h * HEAD_DIM_HGRN, (h + 1) * HEAD_DIM_HGRN)
            chains.append((qf_ref[rf, hl], vf_ref[rf, hl], zf_ref[rf, hl], lbf_ref[:, hl], sf.at[h],
                           tril, tril_bf, C - 1))
            dests.append((of_ref, rf, hl))
            chains.append((qb_ref[rb, hl], vb_ref[rb, hl], zb_ref[rb, hl], lbb_ref[:, hl], sb.at[h],
                           triu, triu_bf, 0))
            dests.append((ob_ref, rb, hl))
        for (ref, rows, hl), o in zip(dests, _hgrn_chunks(chains)):
            ref[rows, hl] = o


def _hgrn(hh, lb_fwd, lb_bwd, batch, seq_len, tok0):
    TS = TS_HGRN
    nT = seq_len // TS
    blk0 = tok0 // TS
    HG = HGRN_HEADS_PER_STEP
    GW = HG * HEAD_DIM_HGRN
    ngroups = N_HEADS_HGRN // HG
    qcol, zfcol, zbcol, vcol = 0, ngroups, 2 * ngroups, 3 * ngroups

    def fspec(col0):
        return pl.BlockSpec((TS, GW), lambda b, hd, i: (blk0 + b * nT + i, col0 + hd))

    def bspec(col0):
        return pl.BlockSpec((TS, GW), lambda b, hd, i: (blk0 + b * nT + nT - 1 - i, col0 + hd))

    lbspec = pl.BlockSpec((1, GW), lambda b, hd, i: (0, hd))
    out_sds = jax.ShapeDtypeStruct((batch * seq_len, D_HGRN), F32)
    return pl.pallas_call(
        _hgrn_body,
        grid=(batch, ngroups, nT),
        in_specs=[fspec(qcol), fspec(vcol), fspec(zfcol), bspec(qcol), bspec(vcol), bspec(zbcol), lbspec, lbspec],
        out_specs=[pl.BlockSpec((TS, GW), lambda b, hd, i: (b * nT + i, hd)),
                   pl.BlockSpec((TS, GW), lambda b, hd, i: (b * nT + nT - 1 - i, hd))],
        out_shape=[out_sds, out_sds],
        scratch_shapes=[pltpu.VMEM((HG, HEAD_DIM_HGRN, HEAD_DIM_HGRN), F32) for _ in range(2)],
        compiler_params=_cparams(("parallel", "parallel", "arbitrary")),
        name="hgrn2",
    )(hh, hh, hh, hh, hh, hh, lb_fwd, lb_bwd)


def _layer_norm(y, g, b):
    mu = jnp.mean(y, axis=-1, keepdims=True)
    yc = y - mu
    var = jnp.mean(yc * yc, axis=-1, keepdims=True)
    return yc * lax.rsqrt(var + LN_EPS) * g + b


def _split_bf16(a):
    hi = a.astype(BF16)
    lo = (a - hi.astype(F32)).astype(BF16)
    return hi, lo


def _outproj_body(attn_a, attn_b, of_a, of_b, ob_a, ob_b, x_a, x_b, g_ref, wo_ref, ng_ref, l1g_ref, l1b_ref,
                  wrh_ref, wrl_ref, br_ref, x1_ref, x1t_ref, idx_ref, gw_ref, *, steps_a):
    from_a = pl.program_id(0) < steps_a
    pick = lambda a_ref, b_ref: jnp.where(from_a, a_ref[...], b_ref[...])
    o = pick(of_a, of_b) + pick(ob_a, ob_b)
    attn = pick(attn_a, attn_b)
    x = pick(x_a, x_b)
    parts = []
    for hd in range(N_HEADS_HGRN):
        oh = o[:, hd * HEAD_DIM_HGRN:(hd + 1) * HEAD_DIM_HGRN]
        ms = jnp.mean(oh * oh, axis=-1, keepdims=True)
        parts.append(oh * lax.rsqrt(ms + NORM_EPS))
    g = g_ref[...]
    hg = jnp.concatenate(parts, axis=-1) * ng_ref[...] * (g * jax.nn.sigmoid(g))
    mix = jnp.dot(attn, wo_ref[:D_ATTN, :], preferred_element_type=F32)
    mix = mix + jnp.dot(hg.astype(BF16), wo_ref[D_ATTN:, :], preferred_element_type=F32)
    x1 = _layer_norm(DEEPNORM_ALPHA * x + mix, l1g_ref[...], l1b_ref[...])
    x1_ref[...] = x1
    for j in range(N_SLAB):
        x1t_ref[pl.ds(j, TM_OUT, stride=N_SLAB), :] = x1[:, j * LANES:(j + 1) * LANES]

    xh, xl = _split_bf16(x1)
    logits = (jnp.dot(xh, wrh_ref[...], preferred_element_type=F32)
              + jnp.dot(xh, wrl_ref[...], preferred_element_type=F32)
              + jnp.dot(xl, wrh_ref[...], preferred_element_type=F32)) + br_ref[...]
    lane = lax.broadcasted_iota(jnp.int32, logits.shape, 1)
    vals, idxs = [], []
    l = logits
    for _ in range(TOP_K):
        m = jnp.max(l, axis=-1, keepdims=True)
        ik = jnp.min(jnp.where(l == m, lane, LANES), axis=-1, keepdims=True)
        vals.append(m)
        idxs.append(ik)
        l = jnp.where(lane == ik, -jnp.inf, l)
    es = [jnp.exp(v - vals[0]) for v in vals]
    den = es[0] + es[1] + es[2] + es[3]
    idx_out = jnp.zeros(logits.shape, jnp.int32)
    gw_out = jnp.zeros(logits.shape, F32)
    for k in range(TOP_K):
        idx_out = jnp.where(lane == k, idxs[k], idx_out)
        gw_out = jnp.where(lane == k, es[k] / den, gw_out)
    idx_ref[...] = idx_out
    gw_ref[...] = gw_out


def _outproj(attn_ab, of_ab, ob_ab, x_ab, hh, wo_bf16, ng_tiled, l1g, l1b, wr_hi, wr_lo, br_pad):
    T = hh.shape[0]
    TM = TM_OUT
    steps_a = x_ab[0].shape[0] // TM
    gcol = 4
    row = lambda w: pl.BlockSpec((TM, w), lambda i: (i, 0))
    full = lambda a: pl.BlockSpec(a.shape, lambda i: (0,) * a.ndim)
    two = lambda w: _two_source_specs((TM, w), steps_a)
    return pl.pallas_call(
        functools.partial(_outproj_body, steps_a=steps_a),
        grid=(T // TM,),
        in_specs=[*two(D_ATTN), *two(D_HGRN), *two(D_HGRN), *two(D_MODEL),
                  pl.BlockSpec((TM, D_HGRN), lambda i: (i, gcol)),
                  full(wo_bf16), full(ng_tiled), full(l1g), full(l1b),
                  full(wr_hi), full(wr_lo), full(br_pad)],
        out_specs=[row(D_MODEL), pl.BlockSpec((TM * N_SLAB, LANES), lambda i: (i, 0)),
                   row(LANES), row(LANES)],
        out_shape=[jax.ShapeDtypeStruct((T, D_MODEL), F32),
                   jax.ShapeDtypeStruct((T * N_SLAB, LANES), F32),
                   jax.ShapeDtypeStruct((T, LANES), jnp.int32),
                   jax.ShapeDtypeStruct((T, LANES), F32)],
        compiler_params=_cparams(("parallel",)),
        name="outproj_ln_router",
    )(*attn_ab, *of_ab, *ob_ab, *x_ab, hh, wo_bf16, ng_tiled, l1g, l1b, wr_hi, wr_lo, br_pad)


def _moe_body(tile_e_ref, n_used_ref, tokc_ref, tok1_ref, tokn_ref, x1t_hbm, wg_ref, bg_ref, wu_ref, bu_ref, wd_ref, bd_ref,
              y_ref, xbuf, wgb, wub, wdb, sem):
    t = pl.program_id(0)
    n_used = n_used_ref[0]
    slot = lax.rem(t, MOE_SLOTS)

    def row_copy(tok_ref, r, dst):
        src = pl.ds(pl.multiple_of(tok_ref[0, 0, r] * N_SLAB, N_SLAB), N_SLAB)
        rows = pl.ds(pl.multiple_of(r * N_SLAB, N_SLAB), N_SLAB)
        return pltpu.make_async_copy(x1t_hbm.at[src, :], xbuf.at[dst, rows, :], sem.at[dst])

    def wait_rows(dst):
        pltpu.make_async_copy(x1t_hbm.at[pl.ds(0, TM_MOE * N_SLAB), :], xbuf.at[dst], sem.at[dst]).wait()

    @pl.when(t == 0)
    def _():
        def issue(r, c):
            row_copy(tokc_ref, r, 0).start()
            row_copy(tok1_ref, r, 1).start()
            return c
        lax.fori_loop(0, TM_MOE, issue, 0)

    @pl.when(t < n_used + (MOE_SLOTS - 1))
    def _():
        wait_rows(slot)

    new_expert = jnp.logical_or(t == 0, tile_e_ref[t] != tile_e_ref[jnp.maximum(t - 1, 0)])

    @pl.when(jnp.logical_and(t < n_used, new_expert))
    def _():
        rows_per = D_MODEL // N_SLAB
        for src, dst in ((wg_ref, wgb), (wu_ref, wub), (wd_ref, wdb)):
            for c in range(N_SLAB):
                rows = slice(c * rows_per, (c + 1) * rows_per)
                dst[rows, :] = src[0, rows, :].astype(BF16)

    @pl.when(t < n_used)
    def _():
        nxt = lax.rem(t + (MOE_SLOTS - 1), MOE_SLOTS)
        xcur = xbuf.at[slot]
        xb = jnp.concatenate([xcur[pl.ds(j, TM_MOE, stride=N_SLAB), :] for j in range(N_SLAB)],
                             axis=1).astype(BF16)
        per = TM_MOE // (2 * MOE_CHUNKS)
        nc = D_FF // MOE_CHUNKS
        hs = []
        for c in range(MOE_CHUNKS):
            for r in range(c * per, (c + 1) * per):
                row_copy(tokn_ref, r, nxt).start()
            cols = slice(c * nc, (c + 1) * nc)
            gt = jnp.minimum(jnp.dot(xb, wgb[:, cols], preferred_element_type=F32) + bg_ref[0, :, cols],
                             SWIGLU_LIMIT)
            up = jnp.clip(jnp.dot(xb, wub[:, cols], preferred_element_type=F32) + bu_ref[0, :, cols],
                          -SWIGLU_LIMIT, SWIGLU_LIMIT)
            hs.append(((up + 1.0) * gt * jax.nn.sigmoid(SWIGLU_ALPHA * gt)).astype(BF16))
        hdn = jnp.concatenate(hs, axis=1)
        nco = D_MODEL // MOE_CHUNKS
        for c in range(MOE_CHUNKS):
            for r in range((MOE_CHUNKS + c) * per, (MOE_CHUNKS + c + 1) * per):
                row_copy(tokn_ref, r, nxt).start()
            cols = slice(c * nco, (c + 1) * nco)
            y = jnp.dot(hdn, wdb[:, cols], preferred_element_type=F32) + bd_ref[0, :, cols]
            for jj in range(nco // LANES):
                y_ref[pl.ds(c * (nco // LANES) + jj, TM_MOE, stride=N_SLAB), :] = y[:, jj * LANES:(jj + 1) * LANES]

    @pl.when(t >= n_used)
    def _():
        y_ref[...] = jnp.zeros_like(y_ref)


def _moe_experts(x1t, tile_e, n_used, tok_sorted, wg, bg, wu, bu, wd, bd):
    n_tiles = tile_e.shape[0]
    wspec = pl.BlockSpec((1, D_MODEL, D_FF), lambda t, te, nu: (te[t], 0, 0))
    bspec = pl.BlockSpec((1, 1, D_FF), lambda t, te, nu: (te[t], 0, 0))

    def tokspec(ahead):
        return pl.BlockSpec((1, 1, TM_MOE), lambda t, te, nu: (jnp.minimum(t + ahead, n_tiles - 1), 0, 0),
                            memory_space=pltpu.SMEM)

    grid_spec = pltpu.PrefetchScalarGridSpec(
        num_scalar_prefetch=2,
        grid=(n_tiles,),
        in_specs=[tokspec(0), tokspec(1), tokspec(MOE_SLOTS - 1),
                  pl.BlockSpec(memory_space=pl.ANY),
                  wspec, bspec, wspec, bspec, wspec, bspec],
        out_specs=pl.BlockSpec((TM_MOE * N_SLAB, LANES), lambda t, te, nu: (t, 0)),
        scratch_shapes=[pltpu.VMEM((MOE_SLOTS, TM_MOE * N_SLAB, LANES), F32),
                        pltpu.VMEM((D_MODEL, D_FF), BF16), pltpu.VMEM((D_MODEL, D_FF), BF16),
                        pltpu.VMEM((D_FF, D_MODEL), BF16),
                        pltpu.SemaphoreType.DMA((MOE_SLOTS,))],
    )
    tok_tiles = tok_sorted.reshape(n_tiles, 1, TM_MOE)
    return pl.pallas_call(
        _moe_body,
        grid_spec=grid_spec,
        out_shape=jax.ShapeDtypeStruct((n_tiles * TM_MOE * N_SLAB, LANES), F32),
        compiler_params=_cparams(("arbitrary",)),
        name="moe_experts",
    )(tile_e, n_used, tok_tiles, tok_tiles, tok_tiles, x1t, wg, bg, wu, bu, wd, bd)


def _combine_body(posc_ref, posn_ref, x1_ref, gw_ref, l2g_ref, l2b_ref, y_hbm, oa_ref, ob_ref, ybuf, zbuf, sem,
                  *, steps_a):
    TM = TM_COMB
    i = pl.program_id(0)
    slot = lax.rem(i, 2)
    nxt = 1 - slot

    def row_copy(pos_ref, k, r, dst):
        src = pl.ds(pl.multiple_of(pos_ref[0, 0, k * TM + r] * N_SLAB, N_SLAB), N_SLAB)
        rows = pl.ds(pl.multiple_of(r * N_SLAB, N_SLAB), N_SLAB)
        return pltpu.make_async_copy(y_hbm.at[src, :], ybuf.at[dst, k, rows, :], sem.at[dst])

    def wait_rows(dst):
        for k in range(TOP_K):
            pltpu.make_async_copy(y_hbm.at[pl.ds(0, TM * N_SLAB), :], ybuf.at[dst, k], sem.at[dst]).wait()

    @pl.when(i == 0)
    def _():
        def issue(r, c):
            for k in range(TOP_K):
                row_copy(posc_ref, k, r, 0).start()
            return c
        lax.fori_loop(0, TM, issue, 0)

    wait_rows(slot)
    gw = gw_ref[...]
    per = TM // N_SLAB
    gws = [gw[:, k:k + 1] for k in range(TOP_K)]
    s1 = jnp.zeros((TM, 1), F32)
    for j in range(N_SLAB):
        for r in range(j * per, (j + 1) * per):
            for k in range(TOP_K):
                row_copy(posn_ref, k, r, nxt).start()
        cols = slice(j * LANES, (j + 1) * LANES)
        z = DEEPNORM_ALPHA * x1_ref[:, cols]
        for k in range(TOP_K):
            z = z + gws[k] * ybuf.at[slot, k][pl.ds(j, TM, stride=N_SLAB), :]
        zbuf[:, cols] = z
        s1 = s1 + jnp.sum(z, axis=-1, keepdims=True)
    mu = s1 * (1.0 / D_MODEL)
    s2 = jnp.zeros((TM, 1), F32)
    for j in range(N_SLAB):
        zc = zbuf[:, j * LANES:(j + 1) * LANES] - mu
        s2 = s2 + jnp.sum(zc * zc, axis=-1, keepdims=True)
    rstd = lax.rsqrt(s2 * (1.0 / D_MODEL) + LN_EPS)
    for o_ref, pred in ((oa_ref, i < steps_a), (ob_ref, i >= steps_a)):
        @pl.when(pred)
        def _(o_ref=o_ref):
            for j in range(N_SLAB):
                cols = slice(j * LANES, (j + 1) * LANES)
                o_ref[:, cols] = (zbuf[:, cols] - mu) * rstd * l2g_ref[:, cols] + l2b_ref[:, cols]

    @pl.when(i == pl.num_programs(0) - 1)
    def _():
        wait_rows(nxt)


def _combine(pos_tiles, x1, gw, l2g, l2b, y_sorted, tokens_a):
    T = x1.shape[0]
    TM = TM_COMB
    n = T // TM
    steps_a = tokens_a // TM
    full = lambda a: pl.BlockSpec(a.shape, lambda i: (0,) * a.ndim)
    return pl.pallas_call(
        functools.partial(_combine_body, steps_a=steps_a),
        grid=(n,),
        in_specs=[pl.BlockSpec((1, 1, TOP_K * TM), lambda i: (i, 0, 0), memory_space=pltpu.SMEM),
                  pl.BlockSpec((1, 1, TOP_K * TM), lambda i: (jnp.minimum(i + 1, n - 1), 0, 0),
                               memory_space=pltpu.SMEM),
                  pl.BlockSpec((TM, D_MODEL), lambda i: (i, 0)),
                  pl.BlockSpec((TM, LANES), lambda i: (i, 0)),
                  full(l2g), full(l2b),
                  pl.BlockSpec(memory_space=pl.ANY)],
        out_specs=list(_two_source_specs((TM, D_MODEL), steps_a)),
        out_shape=[jax.ShapeDtypeStruct((tokens_a, D_MODEL), F32),
                   jax.ShapeDtypeStruct((T - tokens_a, D_MODEL), F32)],
        scratch_shapes=[pltpu.VMEM((2, TOP_K, TM * N_SLAB, LANES), F32), pltpu.VMEM((TM, D_MODEL), F32),
                        pltpu.SemaphoreType.DMA((2,))],
        compiler_params=_cparams(("arbitrary",)),
        name="combine_ln",
    )(pos_tiles, pos_tiles, x1, gw, l2g, l2b, y_sorted)


def _route(top_idx, n_tiles):
    T = top_idx.shape[0]
    n_slots = T * TOP_K
    e = top_idx.reshape(n_slots)
    onehot = (e[:, None] == jnp.arange(N_EXPERTS, dtype=jnp.int32)[None, :]).astype(jnp.int32)
    csum = jnp.cumsum(onehot, axis=0)
    rank = jnp.sum(csum * onehot, axis=1) - 1
    counts = csum[-1]
    tiles_e = (counts + TM_MOE - 1) // TM_MOE
    tile_end = jnp.cumsum(tiles_e)
    tile_start = tile_end - tiles_e
    pos = jnp.sum(onehot * (tile_start * TM_MOE)[None, :], axis=1) + rank
    n_used = tile_end[-1:]
    tile_ids = jnp.arange(n_tiles, dtype=jnp.int32)
    tile_e = jnp.sum((tile_ids[:, None] >= tile_end[None, :]).astype(jnp.int32), axis=1)
    tile_e = jnp.minimum(tile_e, N_EXPERTS - 1)
    n_fill = n_tiles * TM_MOE - n_slots
    gap = tiles_e * TM_MOE - counts
    gap_end = jnp.cumsum(gap)
    d = jnp.arange(n_fill, dtype=jnp.int32)
    ge = jnp.sum((d[:, None] >= gap_end[None, :]).astype(jnp.int32), axis=1)
    ge_hot = (ge[:, None] == jnp.arange(N_EXPERTS, dtype=jnp.int32)[None, :]).astype(jnp.int32)
    in_group = jnp.sum(ge_hot * (tile_start * TM_MOE + counts - (gap_end - gap))[None, :], axis=1) + d
    fill_pos = jnp.where(ge < N_EXPERTS, in_group, n_used[0] * TM_MOE + d - gap_end[-1])
    keys = jnp.concatenate([pos, fill_pos])
    vals = jnp.concatenate([jnp.arange(n_slots, dtype=jnp.int32) // TOP_K, jnp.zeros((n_fill,), jnp.int32)])
    _, tok_sorted = lax.sort((keys, vals), num_keys=1)
    return pos, tok_sorted, tile_e.astype(jnp.int32), n_used.astype(jnp.int32)


def _moe_ln(x1, x1t, top_idx, gw, wg, bg, wu, bu, wd, bd, l2g, l2b, tokens_a):
    T = x1.shape[0]
    n_tiles = T * TOP_K // TM_MOE + N_EXPERTS + (MOE_SLOTS - 1)
    pos, tok_sorted, tile_e, n_used = _route(top_idx, n_tiles)
    y_sorted = _moe_experts(x1t, tile_e, n_used, tok_sorted, wg, bg, wu, bu, wd, bd)
    pos_tiles = pos.reshape(T // TM_COMB, TM_COMB, TOP_K).transpose(0, 2, 1).reshape(T // TM_COMB, 1, TOP_K * TM_COMB)
    return _combine(pos_tiles, x1, gw, l2g, l2b, y_sorted, tokens_a)


def _mixer_ln_router(xs, w_in_bf16, wo_bf16, bias_tabs, lb_f, lb_b, ng_tiled, l1g, l1b, wr_hi, wr_lo, br_pad):
    x2d = [x.reshape(x.shape[0] * x.shape[1], D_MODEL) for x in xs]
    q16, kv16, kvn, hh = _inproj(x2d[0], x2d[1], w_in_bf16)
    attn, o_f, o_b = [], [], []
    tok0 = 0
    for x in xs:
        batch, seq_len, _ = x.shape
        attn.append(_attention(q16, kv16, kvn, bias_tabs, batch, seq_len, tok0 // TQ_ATTN))
        f, b = _hgrn(hh, lb_f, lb_b, batch, seq_len, tok0)
        o_f.append(f)
        o_b.append(b)
        tok0 += batch * seq_len
    return _outproj(attn, o_f, o_b, x2d, hh, wo_bf16, ng_tiled, l1g, l1b, wr_hi, wr_lo, br_pad)


def kernel(x_prompt, x_sample, w_in, w_out, rel_bias, hgrn_lb_fwd, hgrn_lb_bwd, hgrn_norm_g, ln1_g, ln1_b,
           w_router, b_router, w_e_gate, b_e_gate, w_e_up, b_e_up, w_e_down, b_e_down, ln2_g, ln2_b):
    assert DEPTH == 1
    l = 0
    lb_f = jnp.cumsum(jax.nn.softmax(hgrn_lb_fwd.astype(F32), axis=0), axis=0)[l].reshape(1, D_HGRN)
    lb_b = jnp.cumsum(jax.nn.softmax(hgrn_lb_bwd.astype(F32), axis=0), axis=0)[l].reshape(1, D_HGRN)
    col_scale = jnp.concatenate([jnp.full((D_ATTN,), HEAD_DIM_ATTN ** -0.5, F32), jnp.ones((D_IN - D_ATTN,), F32)])
    w_in_bf16 = (w_in[l] * col_scale).astype(BF16)
    wo_bf16 = w_out[l].astype(BF16)
    bias_tabs = _attn_bias_tables(rel_bias)
    ng_tiled = jnp.tile(hgrn_norm_g[l].astype(F32), N_HEADS_HGRN).reshape(1, D_HGRN)
    row = lambda a: a.astype(F32).reshape(1, -1)
    wr = jnp.pad(w_router[l].astype(F32), ((0, 0), (0, LANES - N_EXPERTS)))
    wr_hi, wr_lo = _split_bf16(wr)
    br_pad = jnp.pad(row(b_router[l]), ((0, 0), (0, LANES - N_EXPERTS)), constant_values=NEG_BIG)

    x1, x1t, idx, gw = _mixer_ln_router((x_prompt, x_sample), w_in_bf16, wo_bf16, bias_tabs, lb_f, lb_b, ng_tiled,
                                        row(ln1_g[l]), row(ln1_b[l]), wr_hi, wr_lo, br_pad)
    n_p = x_prompt.shape[0] * x_prompt.shape[1]
    y_p, y_s = _moe_ln(x1, x1t, idx[:, :TOP_K], gw,
                       w_e_gate[l].astype(F32), b_e_gate[l].astype(F32).reshape(N_EXPERTS, 1, D_FF),
                       w_e_up[l].astype(F32), b_e_up[l].astype(F32).reshape(N_EXPERTS, 1, D_FF),
                       w_e_down[l].astype(F32), b_e_down[l].astype(F32).reshape(N_EXPERTS, 1, D_MODEL),
                       row(ln2_g[l]), row(ln2_b[l]), n_p)
    return (y_p.reshape(x_prompt.shape), y_s.reshape(x_sample.shape))
```
